```python
import math
import jax, jax.numpy as jnp
from jax import lax
import numpy as np

D_MODEL = 1024
BATCH = 8
SEQ = 2048
DEPTH = 2

Q_BLOCK = 128
NORM_EPS = 1e-6
SB_HEADS = 8
SB_HEAD_DIM = 64
MLA_HEADS = 8
MLA_Q_RANK = 768
MLA_KV_RANK = 256
MLA_NOPE_DIM = 64
MLA_ROPE_DIM = 32
MLA_V_DIM = 64
MLA_QK_DIM = MLA_NOPE_DIM + MLA_ROPE_DIM
ROPE_THETA = 10000.0
DIFF_HEADS = 4
DIFF_HEAD_DIM = 64
DIFF_V_DIM = 2 * DIFF_HEAD_DIM
N_BRANCH = 3
BRANCH_WIDTH = 512
D_FF = 2816
N_EXPERTS = 8
TOP_K = 2
EXPERT_D_FF = 2816
N_ADA = 6
SB_COLS = 3 * SB_HEADS * SB_HEAD_DIM
MLA_COLS = MLA_Q_RANK + MLA_KV_RANK + MLA_ROPE_DIM
DIFF_QK_COLS = DIFF_HEADS * 2 * DIFF_HEAD_DIM
DIFF_COLS = 2 * DIFF_QK_COLS + DIFF_HEADS * DIFF_V_DIM
GATE_COLS = N_BRANCH * D_MODEL
IN_COLS = SB_COLS + MLA_COLS + DIFF_COLS + GATE_COLS

kernel_name = 'hybrid_sb_mla_diff_moe_block'


def _rmsnorm(x, g):
    xf = x.astype(jnp.float32)
    y = xf * lax.rsqrt(jnp.mean(xf * xf, axis=-1, keepdims=True) + NORM_EPS)
    return y.astype(x.dtype) * g


def _to_blocks(a):
    b, s = a.shape[:2]
    return jnp.moveaxis(a.reshape(b, s // Q_BLOCK, Q_BLOCK, *a.shape[2:]), 1, 0)


def _from_blocks(a):
    nb, b, qb = a.shape[:3]
    return jnp.moveaxis(a, 0, 1).reshape(b, nb * qb, *a.shape[3:])


def _rope(x, pos):
    half = x.shape[-1] // 2
    inv_freq = ROPE_THETA ** (-jnp.arange(half, dtype=jnp.float32) / half)
    ang = pos.astype(jnp.float32)[..., None] * inv_freq
    ang = ang.reshape(ang.shape[:2] + (1,) * (x.ndim - 3) + (half,))
    cos, sin = jnp.cos(ang), jnp.sin(ang)
    x1 = x[..., :half].astype(jnp.float32)
    x2 = x[..., half:].astype(jnp.float32)
    return jnp.concatenate([x1 * cos - x2 * sin, x1 * sin + x2 * cos], axis=-1).astype(x.dtype)


def _stick_breaking(q, k, v):
    scale = 1.0 / math.sqrt(q.shape[-1])
    key_idx = jnp.arange(k.shape[1])

    def block(args):
        qb, i = args
        z = jnp.einsum('bqhd,bkhd->bhqk', qb, k).astype(jnp.float32) * scale
        q_idx = i * Q_BLOCK + jnp.arange(Q_BLOCK)
        strict = key_idx[None, :] < q_idx[:, None]
        log_1m_beta = jnp.where(strict, jax.nn.log_sigmoid(-z), 0.0)
        later = lax.cumsum(log_1m_beta, axis=3, reverse=True) - log_1m_beta
        w = jnp.where(strict, jnp.exp(jax.nn.log_sigmoid(z) + later), 0.0)
        return jnp.einsum('bhqk,bkhd->bqhd', w.astype(v.dtype), v)

    nb = q.shape[1] // Q_BLOCK
    return _from_blocks(lax.map(block, (_to_blocks(q), jnp.arange(nb))))


def _causal_softmax_attention(q, k, v):
    scale = 1.0 / math.sqrt(q.shape[-1])
    key_idx = jnp.arange(k.shape[1])

    def block(args):
        qb, i = args
        sc = jnp.einsum('bqhd,bkhd->bhqk', qb, k).astype(jnp.float32) * scale
        q_idx = i * Q_BLOCK + jnp.arange(Q_BLOCK)
        causal = key_idx[None, :] <= q_idx[:, None]
        p = jax.nn.softmax(jnp.where(causal, sc, -jnp.inf), axis=-1)
        return jnp.einsum('bhqk,bkhd->bqhd', p.astype(v.dtype), v)

    nb = q.shape[1] // Q_BLOCK
    return _from_blocks(lax.map(block, (_to_blocks(q), jnp.arange(nb))))


def _diff_attention(q, k, v, pos, slopes, lam):
    scale = 1.0 / math.sqrt(q.shape[-1])
    key_idx = jnp.arange(k.shape[1])

    def block(args):
        qb, pq, i = args
        sc = jnp.einsum('bqhmd,bkhmd->bhmqk', qb, k).astype(jnp.float32) * scale
        dist = jnp.abs(pq[:, :, None] - pos[:, None, :]).astype(jnp.float32)
        sc = sc - slopes[None, :, None, None, None] * dist[:, None, None]
        q_idx = i * Q_BLOCK + jnp.arange(Q_BLOCK)
        causal = key_idx[None, :] <= q_idx[:, None]
        p = jax.nn.softmax(jnp.where(causal, sc, -jnp.inf), axis=-1)
        a = p[:, :, 0] - lam * p[:, :, 1]
        return jnp.einsum('bhqk,bkhd->bqhd', a.astype(v.dtype), v)

    nb = q.shape[1] // Q_BLOCK
    return _from_blocks(lax.map(block, (_to_blocks(q), _to_blocks(pos), jnp.arange(nb))))


def _mixer(h, pos, layer, w_in, mla_q_norm, w_mla_uq, mla_kv_norm, w_mla_ukv, mla_q_gain,
           mla_k_gain, diff_q_gain, diff_k_gain, diff_lambda, diff_subln, w_branch, w_out):
    b, s, _ = h.shape
    proj = h @ w_in
    sb, mla, dif, gate_logits = jnp.split(
        proj, [SB_COLS, SB_COLS + MLA_COLS, SB_COLS + MLA_COLS + DIFF_COLS], axis=-1)

    sb = sb.reshape(b, s, 3, SB_HEADS, SB_HEAD_DIM)
    o_sb = _stick_breaking(sb[:, :, 0], sb[:, :, 1], sb[:, :, 2]).reshape(b, s, BRANCH_WIDTH)

    c_q, c_kv, k_rope = jnp.split(mla, [MLA_Q_RANK, MLA_Q_RANK + MLA_KV_RANK], axis=-1)
    q = (_rmsnorm(c_q, mla_q_norm) @ w_mla_uq).reshape(b, s, MLA_HEADS, MLA_QK_DIM)
    kv = (_rmsnorm(c_kv, mla_kv_norm) @ w_mla_ukv).reshape(b, s, MLA_HEADS, MLA_NOPE_DIM + MLA_V_DIM)
    k_rope = jnp.broadcast_to(_rope(k_rope, pos)[:, :, None, :], (b, s, MLA_HEADS, MLA_ROPE_DIM))
    q = jnp.concatenate([q[..., :MLA_NOPE_DIM], _rope(q[..., MLA_NOPE_DIM:], pos)], axis=-1)
    k = jnp.concatenate([kv[..., :MLA_NOPE_DIM], k_rope], axis=-1)
    o_mla = _causal_softmax_attention(_rmsnorm(q, mla_q_gain), _rmsnorm(k, mla_k_gain),
                                      kv[..., MLA_NOPE_DIM:]).reshape(b, s, BRANCH_WIDTH)

    dq, dk, dv = jnp.split(dif, [DIFF_QK_COLS, 2 * DIFF_QK_COLS], axis=-1)
    dq = _rmsnorm(dq.reshape(b, s, DIFF_HEADS, 2, DIFF_HEAD_DIM), diff_q_gain)
    dk = _rmsnorm(dk.reshape(b, s, DIFF_HEADS, 2, DIFF_HEAD_DIM), diff_k_gain)
    dv = dv.reshape(b, s, DIFF_HEADS, DIFF_V_DIM)
    lam_init = 0.8 - 0.6 * math.exp(-0.3 * layer)
    lp = diff_lambda.astype(jnp.float32)
    lam = jnp.exp(jnp.sum(lp[0] * lp[1])) - jnp.exp(jnp.sum(lp[2] * lp[3])) + lam_init
    slopes = 2.0 ** (-8.0 * jnp.arange(1, DIFF_HEADS + 1, dtype=jnp.float32) / DIFF_HEADS)
    o_diff = _diff_attention(dq, dk, dv, pos, slopes, lam)
    o_diff = (_rmsnorm(o_diff, diff_subln) * (1.0 - lam_init)).reshape(b, s, BRANCH_WIDTH)

    branches = jnp.stack([o_sb, o_mla, o_diff], axis=2)
    gates = jax.nn.sigmoid(gate_logits.reshape(b, s, N_BRANCH, D_MODEL))
    merged = jnp.sum(gates * jnp.einsum('bsnc,ncd->bsnd', branches, w_branch), axis=2)
    return merged @ w_out


def _swiglu(h, w_gate, w_up, w_down):
    return (jax.nn.silu(h @ w_gate) * (h @ w_up)) @ w_down


def _moe(h, w_router, w_gate, w_up, w_down):
    logits = (h @ w_router).astype(jnp.float32)
    top_val, top_idx = lax.top_k(logits, TOP_K)
    top_w = jax.nn.softmax(top_val, axis=-1)
    combine = jnp.sum(jax.nn.one_hot(top_idx, N_EXPERTS, dtype=jnp.float32) * top_w[..., None], axis=-2)
    y = jnp.zeros_like(h)
    for e in range(N_EXPERTS):
        y = y + combine[..., e:e + 1].astype(h.dtype) * _swiglu(h, w_gate[e], w_up[e], w_down[e])
    return y


def setup_inputs(seed: int = 0) -> dict:
    key = jax.random.key(seed)
    ks = iter(jax.random.split(key, 40))
    nd = (DEPTH + 1) // 2
    nm = DEPTH // 2

    def nrm(shape, fan_in):
        return jax.random.normal(next(ks), shape, jnp.float32) * fan_in ** -0.5

    def gain(shape):
        return 1.0 + 0.02 * jax.random.normal(next(ks), shape, jnp.float32)

    x = jax.random.normal(next(ks), (BATCH, SEQ, D_MODEL), jnp.float32)
    c = jax.random.normal(next(ks), (BATCH, D_MODEL), jnp.float32)
    offset = jax.random.randint(next(ks), (BATCH, 1), 0, 1024, dtype=jnp.int32)
    positions = offset + jnp.arange(SEQ, dtype=jnp.int32)[None, :]
    return {
        'x': x,
        'c': c,
        'positions': positions,
        'norm1_g': gain((DEPTH, D_MODEL)),
        'norm2_g': gain((DEPTH, D_MODEL)),
        'w_ada': nrm((DEPTH, D_MODEL, N_ADA * D_MODEL), D_MODEL),
        'b_ada': 0.02 * jax.random.normal(next(ks), (DEPTH, N_ADA * D_MODEL), jnp.float32),
        'w_in': nrm((DEPTH, D_MODEL, IN_COLS), D_MODEL),
        'mla_q_norm': gain((DEPTH, MLA_Q_RANK)),
        'w_mla_uq': nrm((DEPTH, MLA_Q_RANK, MLA_HEADS * MLA_QK_DIM), MLA_Q_RANK),
        'mla_kv_norm': gain((DEPTH, MLA_KV_RANK)),
        'w_mla_ukv': nrm((DEPTH, MLA_KV_RANK, MLA_HEADS * (MLA_NOPE_DIM + MLA_V_DIM)), MLA_KV_RANK),
        'mla_q_gain': gain((DEPTH, MLA_QK_DIM)),
        'mla_k_gain': gain((DEPTH, MLA_QK_DIM)),
        'diff_q_gain': gain((DEPTH, DIFF_HEAD_DIM)),
        'diff_k_gain': gain((DEPTH, DIFF_HEAD_DIM)),
        'diff_lambda': 0.1 * jax.random.normal(next(ks), (DEPTH, 4, DIFF_HEAD_DIM), jnp.float32),
        'diff_subln': gain((DEPTH, DIFF_V_DIM)),
        'w_branch': nrm((DEPTH, N_BRANCH, BRANCH_WIDTH, D_MODEL), BRANCH_WIDTH),
        'w_out': nrm((DEPTH, D_MODEL, D_MODEL), D_MODEL),
        'w_ffn_gate': nrm((nd, D_MODEL, D_FF), D_MODEL),
        'w_ffn_up': nrm((nd, D_MODEL, D_FF), D_MODEL),
        'w_ffn_down': nrm((nd, D_FF, D_MODEL), D_FF),
        'w_router': nrm((nm, D_MODEL, N_EXPERTS), D_MODEL),
        'w_exp_gate': nrm((nm, N_EXPERTS, D_MODEL, EXPERT_D_FF), D_MODEL),
        'w_exp_up': nrm((nm, N_EXPERTS, D_MODEL, EXPERT_D_FF), D_MODEL),
        'w_exp_down': nrm((nm, N_EXPERTS, EXPERT_D_FF, D_MODEL), EXPERT_D_FF),
    }


def reference(x, c, positions, norm1_g, norm2_g, w_ada, b_ada, w_in, mla_q_norm, w_mla_uq,
              mla_kv_norm, w_mla_ukv, mla_q_gain, mla_k_gain, diff_q_gain, diff_k_gain,
              diff_lambda, diff_subln, w_branch, w_out, w_ffn_gate, w_ffn_up, w_ffn_down,
              w_router, w_exp_gate, w_exp_up, w_exp_down):
    cond = jax.nn.silu(c)
    for layer in range(DEPTH):
        mod = (cond @ w_ada[layer] + b_ada[layer])[:, None, :]
        sh1, sc1, g1, sh2, sc2, g2 = jnp.split(mod, N_ADA, axis=-1)

        h = _rmsnorm(x, norm1_g[layer]) * (1.0 + sc1) + sh1
        x = x + g1 * _mixer(h, positions, layer, w_in[layer], mla_q_norm[layer], w_mla_uq[layer],
                            mla_kv_norm[layer], w_mla_ukv[layer], mla_q_gain[layer], mla_k_gain[layer],
                            diff_q_gain[layer], diff_k_gain[layer], diff_lambda[layer],
                            diff_subln[layer], w_branch[layer], w_out[layer])

        h = _rmsnorm(x, norm2_g[layer]) * (1.0 + sc2) + sh2
        j = layer // 2
        if layer % 2 == 0:
            f = _swiglu(h, w_ffn_gate[j], w_ffn_up[j], w_ffn_down[j])
        else:
            f = _moe(h, w_router[j], w_exp_gate[j], w_exp_up[j], w_exp_down[j])
        x = x + g2 * f
    return x
```

```python
import functools
import math

import jax
import jax.numpy as jnp
from jax import lax
from jax.experimental import pallas as pl
from jax.experimental.pallas import tpu as pltpu

F32 = jnp.float32
BF16 = jnp.bfloat16
HIGHEST = lax.Precision.HIGHEST

NORM_EPS = 1e-6
LANES = 128
HALF = 64

SB_HEADS = 8
MLA_HEADS = 8
MLA_Q_RANK = 768
MLA_KV_RANK = 256
MLA_NOPE = 64
MLA_ROPE = 32
MLA_QK = MLA_NOPE + MLA_ROPE
ROPE_THETA = 10000.0
DIFF_HEADS = 4
BRANCH_WIDTH = 512
N_BRANCH = 3
N_EXPERTS = 8
TOP_K = 2
N_ADA = 6
NEG_BIG = -1e30

COL_SB = 0
COL_CQ = 1536
COL_CKV = 2304
COL_DIFF = 2560
COL_GATE = 4096
COL_KROPE = 7168
IN_COLS_PACKED = 7296

VMEM_LIMIT = 56 * 1024 * 1024


def _cparams(*sem):
    return pltpu.CompilerParams(dimension_semantics=sem, vmem_limit_bytes=VMEM_LIMIT)


def _dot(a, b):
    return jnp.dot(a, b, preferred_element_type=F32)


def _dot_nt(a, b):
    return lax.dot_general(a, b, (((1,), (1,)), ((), ())), preferred_element_type=F32)


def _norm_mod(x, g, sc, sh):
    ms = jnp.mean(x * x, axis=-1, keepdims=True)
    return (x * lax.rsqrt(ms + NORM_EPS) * g) * (1.0 + sc) + sh


def _ada_kernel(c_ref, w_ref, b_ref, o_ref):
    c = c_ref[...]
    cond = c * jax.nn.sigmoid(c)
    o_ref[0] = jnp.dot(cond, w_ref[0], precision=HIGHEST, preferred_element_type=F32) + b_ref[0]


def _ada(c, w_ada, b_ada):
    depth, d, n = w_ada.shape
    b = c.shape[0]
    tn = 1536
    return pl.pallas_call(
        _ada_kernel,
        out_shape=jax.ShapeDtypeStruct((depth, b, n), F32),
        grid=(depth, n // tn),
        in_specs=[pl.BlockSpec((b, d), lambda l, j: (0, 0)),
                  pl.BlockSpec((1, d, tn), lambda l, j: (l, 0, j)),
                  pl.BlockSpec((1, 1, tn), lambda l, j: (l, 0, j))],
        out_specs=pl.BlockSpec((1, b, tn), lambda l, j: (l, 0, j)),
        compiler_params=_cparams("arbitrary", "arbitrary"),
        name="ada_mod",
    )(c, w_ada, b_ada.reshape(depth, 1, n))


def _in_proj_kernel(x_ref, g_ref, mod_ref, w_ref, o_ref, h_ref):
    @pl.when(pl.program_id(1) == 0)
    def _():
        h = _norm_mod(x_ref[...], g_ref[...], mod_ref[0, 1:2, :], mod_ref[0, 0:1, :])
        h_ref[...] = h.astype(BF16)

    o_ref[...] = _dot(h_ref[...], w_ref[...]).astype(o_ref.dtype)


def _in_proj(x2, g, mod, w, seq):
    t, d = x2.shape
    n = w.shape[1]
    tm = min(512, seq)
    tn = 2432
    per_b = seq // tm
    return pl.pallas_call(
        _in_proj_kernel,
        out_shape=jax.ShapeDtypeStruct((t, n), BF16),
        grid=(t // tm, n // tn),
        in_specs=[pl.BlockSpec((tm, d), lambda i, j: (i, 0)),
                  pl.BlockSpec((1, d), lambda i, j: (0, 0)),
                  pl.BlockSpec((1, N_ADA, d), lambda i, j: (i // per_b, 0, 0)),
                  pl.BlockSpec((d, tn), lambda i, j: (0, j))],
        out_specs=pl.BlockSpec((tm, tn), lambda i, j: (i, j)),
        scratch_shapes=[pltpu.VMEM((tm, d), BF16)],
        compiler_params=_cparams("arbitrary", "arbitrary"),
        name="in_proj",
    )(x2, g, mod, w)


def _prep_kernel(cq_ref, ckv_ref, kr_ref, dq_ref, dk_ref, pos_ref, tab_ref, qn_ref, kvn_ref,
                 wuq_ref, wuk_ref, wuv_ref, qm_ref, km_ref, vm_ref, dqn_ref, dkn_ref):
    tm = cq_ref.shape[0]
    lane = lax.broadcasted_iota(jnp.int32, (tm, LANES), 1)
    freq = tab_ref[0:1, :]
    sgn_lo = tab_ref[1:2, :]
    sgn_hi = tab_ref[2:3, :]
    gq = tab_ref[3:4, :]
    gk = tab_ref[4:5, :]
    gdq = tab_ref[5:6, :]
    gdk = tab_ref[6:7, :]

    ang = pos_ref[...].astype(F32) * freq
    cos = jnp.cos(ang)
    sin = jnp.sin(ang)
    s_lo = sin * sgn_lo
    s_hi = sin * sgn_hi

    def rope_norm(v, gain):
        r = v * cos + pltpu.roll(v, LANES - MLA_ROPE // 2, 1) * s_lo + pltpu.roll(v, MLA_ROPE // 2, 1) * s_hi
        ms = jnp.sum(r * r, axis=-1, keepdims=True) * (1.0 / MLA_QK)
        return (r * lax.rsqrt(ms + NORM_EPS) * gain).astype(BF16)

    cq = cq_ref[...].astype(F32)
    cqn = cq * lax.rsqrt(jnp.mean(cq * cq, axis=-1, keepdims=True) + NORM_EPS) * qn_ref[...]
    q = _dot(cqn.astype(BF16), wuq_ref[...])
    ckv = ckv_ref[...].astype(F32)
    ckvn = (ckv * lax.rsqrt(jnp.mean(ckv * ckv, axis=-1, keepdims=True) + NORM_EPS) * kvn_ref[...]).astype(BF16)
    kn = _dot(ckvn, wuk_ref[...])
    vm_ref[...] = _dot(ckvn, wuv_ref[...]).astype(BF16)
    kr = kr_ref[...].astype(F32)
    for h in range(MLA_HEADS):
        qm_ref[0, h] = rope_norm(q[:, h * LANES:(h + 1) * LANES], gq)
        km_ref[0, h] = rope_norm(kn[:, h * LANES:(h + 1) * LANES] + kr, gk)

    def half_norm(ref, gain, out_ref):
        for cb in range(ref.shape[1] // LANES):
            v = ref[:, cb * LANES:(cb + 1) * LANES].astype(F32)
            sq = v * v
            lo = jnp.sum(jnp.where(lane < HALF, sq, 0.0), axis=-1, keepdims=True)
            hi = jnp.sum(jnp.where(lane >= HALF, sq, 0.0), axis=-1, keepdims=True)
            ms = jnp.where(lane < HALF, lo, hi) * (1.0 / HALF)
            out_ref[:, cb * LANES:(cb + 1) * LANES] = (v * lax.rsqrt(ms + NORM_EPS) * gain).astype(BF16)

    half_norm(dq_ref, gdq, dqn_ref)
    half_norm(dk_ref, gdk, dkn_ref)


def _prep(proj, pos_col, tab, qn, kvn, wuq, wuk, wuv, batch, seq):
    t = proj.shape[0]
    tm = min(512, seq)
    per_b = seq // tm
    h = MLA_HEADS
    const = lambda shape: pl.BlockSpec(shape, lambda i: (0,) * len(shape))
    return pl.pallas_call(
        _prep_kernel,
        out_shape=(jax.ShapeDtypeStruct((batch, h, seq, LANES), BF16),
                   jax.ShapeDtypeStruct((batch, h, seq, LANES), BF16),
                   jax.ShapeDtypeStruct((t, BRANCH_WIDTH), BF16),
                   jax.ShapeDtypeStruct((t, BRANCH_WIDTH), BF16),
                   jax.ShapeDtypeStruct((t, BRANCH_WIDTH), BF16)),
        grid=(t // tm,),
        in_specs=[pl.BlockSpec((tm, MLA_Q_RANK), lambda i: (i, COL_CQ // MLA_Q_RANK)),
                  pl.BlockSpec((tm, MLA_KV_RANK), lambda i: (i, COL_CKV // MLA_KV_RANK)),
                  pl.BlockSpec((tm, LANES), lambda i: (i, COL_KROPE // LANES)),
                  pl.BlockSpec((tm, BRANCH_WIDTH), lambda i: (i, COL_DIFF // BRANCH_WIDTH)),
                  pl.BlockSpec((tm, BRANCH_WIDTH), lambda i: (i, COL_DIFF // BRANCH_WIDTH + 1)),
                  pl.BlockSpec((tm, 1), lambda i: (i, 0)),
                  const((8, LANES)), const((1, MLA_Q_RANK)), const((1, MLA_KV_RANK)),
                  const((MLA_Q_RANK, h * LANES)), const((MLA_KV_RANK, h * LANES)),
                  const((MLA_KV_RANK, BRANCH_WIDTH))],
        out_specs=(pl.BlockSpec((1, h, tm, LANES), lambda i: (i // per_b, 0, i % per_b, 0)),
                   pl.BlockSpec((1, h, tm, LANES), lambda i: (i // per_b, 0, i % per_b, 0)),
                   pl.BlockSpec((tm, BRANCH_WIDTH), lambda i: (i, 0)),
                   pl.BlockSpec((tm, BRANCH_WIDTH), lambda i: (i, 0)),
                   pl.BlockSpec((tm, BRANCH_WIDTH), lambda i: (i, 0))),
        compiler_params=_cparams("arbitrary"),
        name="attn_prep",
    )(proj, proj, proj, proj, proj, pos_col, tab, qn, kvn, wuq, wuk, wuv)


def _split_dot(a, u):
    hi = a.astype(BF16)
    lo = (a - hi.astype(F32)).astype(BF16)
    return _dot(hi, u) + _dot(lo, u)


def _sb_kernel(q_ref, k_ref, v_ref, o_ref, *, blk):
    qi = pl.program_id(2)
    lane = lax.broadcasted_iota(jnp.int32, (blk, LANES), 1)
    row = lax.broadcasted_iota(jnp.int32, (blk, blk), 0)
    col = lax.broadcasted_iota(jnp.int32, (blk, blk), 1)
    strict = col < row
    upper = jnp.where(row > col, 1.0, 0.0).astype(BF16)
    q = q_ref[...] * jnp.asarray(1.0 / math.sqrt(HALF), BF16)
    qs = (jnp.where(lane < HALF, q, jnp.zeros_like(q)), jnp.where(lane >= HALF, q, jnp.zeros_like(q)))

    def block(j, carry, masked):
        start = pl.multiple_of(j * blk, blk)
        kb = k_ref[pl.ds(start, blk), :]
        vb = v_ref[pl.ds(start, blk), :]
        out = []
        for s in range(2):
            c, acc = carry[s]
            z = _dot_nt(qs[s], kb)
            sp = jnp.log(1.0 + jnp.exp(-jnp.abs(z)))
            lneg = -(jnp.maximum(z, 0.0) + sp)
            lpos = lneg + z
            if masked:
                lneg = jnp.where(strict, lneg, 0.0)
            later = _split_dot(lneg, upper)
            w = jnp.exp(lpos + later + c)
            if masked:
                w = jnp.where(strict, w, 0.0)
            acc = acc + _dot(w.astype(BF16), vb)
            c = c + later[:, 0:1] + lneg[:, 0:1]
            out.append((c, acc))
        return tuple(out)

    zero = (jnp.zeros((blk, 1), F32), jnp.zeros((blk, LANES), F32))
    carry = block(qi, (zero, zero), True)
    carry = lax.fori_loop(0, qi, lambda t, cr: block(qi - 1 - t, cr, False), carry)
    o_ref[...] = jnp.where(lane < HALF, carry[0][1], carry[1][1]).astype(o_ref.dtype)


def _sb_attention(proj, batch, seq):
    t = proj.shape[0]
    blk = min(256, seq)
    nq = seq // blk
    npair = SB_HEADS // 2
    kcol = COL_SB // LANES + npair
    vcol = COL_SB // LANES + 2 * npair
    return pl.pallas_call(
        functools.partial(_sb_kernel, blk=blk),
        out_shape=jax.ShapeDtypeStruct((t, BRANCH_WIDTH), BF16),
        grid=(batch, npair, nq),
        in_specs=[pl.BlockSpec((blk, LANES), lambda b, p, i: (b * nq + i, COL_SB // LANES + p)),
                  pl.BlockSpec((seq, LANES), lambda b, p, i: (b, kcol + p)),
                  pl.BlockSpec((seq, LANES), lambda b, p, i: (b, vcol + p))],
        out_specs=pl.BlockSpec((blk, LANES), lambda b, p, i: (b * nq + i, p)),
        compiler_params=_cparams("arbitrary", "arbitrary", "arbitrary"),
        name="sb_attention",
    )(proj, proj, proj)


def _softmax_update(state, s, vb):
    m, l, acc = state
    m_new = jnp.maximum(m, jnp.max(s, axis=-1, keepdims=True))
    alpha = jnp.exp(m - m_new)
    p = jnp.exp(s - m_new)
    l = alpha * l + jnp.sum(p, axis=-1, keepdims=True)
    acc = alpha * acc + _dot(p.astype(BF16), vb)
    return m_new, l, acc


def _softmax_init(blk, width):
    return (jnp.full((blk, 1), NEG_BIG, F32), jnp.zeros((blk, 1), F32), jnp.zeros((blk, width), F32))


def _mla_kernel(q_ref, k_ref, v_ref, o_ref, *, blk):
    qi = pl.program_id(2)
    lane = lax.broadcasted_iota(jnp.int32, (blk, LANES), 1)
    row = lax.broadcasted_iota(jnp.int32, (blk, blk), 0)
    col = lax.broadcasted_iota(jnp.int32, (blk, blk), 1)
    causal = col <= row
    qs = (q_ref[0, 0], q_ref[0, 1])

    def block(j, carry, masked):
        start = pl.multiple_of(j * blk, blk)
        vb = v_ref[pl.ds(start, blk), :]
        out = []
        for s in range(2):
            sc = _dot_nt(qs[s], k_ref[0, s, pl.ds(start, blk), :])
            if masked:
                sc = jnp.where(causal, sc, NEG_BIG)
            out.append(_softmax_update(carry[s], sc, vb))
        return tuple(out)

    init = _softmax_init(blk, LANES)
    carry = lax.fori_loop(0, qi, lambda j, cr: block(j, cr, False), (init, init))
    carry = block(qi, carry, True)
    o0 = carry[0][2] / carry[0][1]
    o1 = carry[1][2] / carry[1][1]
    o_ref[...] = jnp.where(lane < HALF, o0, o1).astype(o_ref.dtype)


def _mla_attention(qm, km, vm, batch, seq):
    t = vm.shape[0]
    blk = min(256, seq)
    nq = seq // blk
    npair = MLA_HEADS // 2
    return pl.pallas_call(
        functools.partial(_mla_kernel, blk=blk),
        out_shape=jax.ShapeDtypeStruct((t, BRANCH_WIDTH), BF16),
        grid=(batch, npair, nq),
        in_specs=[pl.BlockSpec((1, 2, blk, LANES), lambda b, p, i: (b, p, i, 0)),
                  pl.BlockSpec((1, 2, seq, LANES), lambda b, p, i: (b, p, 0, 0)),
                  pl.BlockSpec((seq, LANES), lambda b, p, i: (b, p))],
        out_specs=pl.BlockSpec((blk, LANES), lambda b, p, i: (b * nq + i, p)),
        compiler_params=_cparams("arbitrary", "arbitrary", "arbitrary"),
        name="mla_attention",
    )(qm, km, vm)


def _diff_kernel(slope_ref, q_ref, k_ref, v_ref, pq_ref, pk_ref, lam_ref, g_ref, o_ref, *, blk, lam_init):
    h = pl.program_id(1)
    qi = pl.program_id(2)
    lane = lax.broadcasted_iota(jnp.int32, (blk, LANES), 1)
    row = lax.broadcasted_iota(jnp.int32, (blk, blk), 0)
    col = lax.broadcasted_iota(jnp.int32, (blk, blk), 1)
    causal = col <= row
    slope = slope_ref[h]
    q = q_ref[...]
    qs = (jnp.where(lane < HALF, q, jnp.zeros_like(q)), jnp.where(lane >= HALF, q, jnp.zeros_like(q)))
    pq = pq_ref[...]

    def block(j, carry, masked):
        start = pl.multiple_of(j * blk, blk)
        kb = k_ref[pl.ds(start, blk), :]
        vb = v_ref[pl.ds(start, blk), :]
        bias = slope * jnp.abs(pq - pk_ref[0, j]).astype(F32)
        out = []
        for s in range(2):
            sc = _dot_nt(qs[s], kb) - bias
            if masked:
                sc = jnp.where(causal, sc, NEG_BIG)
            out.append(_softmax_update(carry[s], sc, vb))
        return tuple(out)

    init = _softmax_init(blk, LANES)
    carry = lax.fori_loop(0, qi, lambda j, cr: block(j, cr, False), (init, init))
    carry = block(qi, carry, True)

    lp = lam_ref[...]
    e1 = jnp.exp(jnp.sum(lp[0:1, :] * lp[1:2, :], axis=-1, keepdims=True))
    e2 = jnp.exp(jnp.sum(lp[2:3, :] * lp[3:4, :], axis=-1, keepdims=True))
    lam = e1 - e2 + lam_init
    o = carry[0][2] / carry[0][1] - lam * (carry[1][2] / carry[1][1])
    ms = jnp.mean(o * o, axis=-1, keepdims=True)
    o_ref[...] = (o * lax.rsqrt(ms + NORM_EPS) * g_ref[...] * (1.0 - lam_init)).astype(o_ref.dtype)


def _diff_attention(dqn, dkn, proj, pos_col, pos_blk, diff_lambda, subln, slopes, lam_init, batch, seq):
    t = dqn.shape[0]
    blk = min(256, seq)
    nq = seq // blk
    vcol = (COL_DIFF + 2 * BRANCH_WIDTH) // LANES
    grid_spec = pltpu.PrefetchScalarGridSpec(
        num_scalar_prefetch=1,
        grid=(batch, DIFF_HEADS, nq),
        in_specs=[pl.BlockSpec((blk, LANES), lambda b, h, i, sl: (b * nq + i, h)),
                  pl.BlockSpec((seq, LANES), lambda b, h, i, sl: (b, h)),
                  pl.BlockSpec((seq, LANES), lambda b, h, i, sl: (b, vcol + h)),
                  pl.BlockSpec((blk, 1), lambda b, h, i, sl: (b * nq + i, 0)),
                  pl.BlockSpec((1, nq, 1, blk), lambda b, h, i, sl: (b, 0, 0, 0)),
                  pl.BlockSpec((4, HALF), lambda b, h, i, sl: (0, 0)),
                  pl.BlockSpec((1, LANES), lambda b, h, i, sl: (0, 0))],
        out_specs=pl.BlockSpec((blk, LANES), lambda b, h, i, sl: (b * nq + i, h)),
    )
    return pl.pallas_call(
        functools.partial(_diff_kernel, blk=blk, lam_init=lam_init),
        out_shape=jax.ShapeDtypeStruct((t, BRANCH_WIDTH), BF16),
        grid_spec=grid_spec,
        compiler_params=_cparams("arbitrary", "arbitrary", "arbitrary"),
        name="diff_attention",
    )(slopes, dqn, dkn, proj, pos_col, pos_blk, diff_lambda, subln)


def _merge_kernel(*refs, moe):
    (osb_ref, omla_ref, odiff_ref, g0_ref, g1_ref, g2_ref, x_ref, mod_ref, wb_ref, wo_ref, ng_ref) = refs[:11]
    if moe:
        wr_ref, xo_ref, h_ref, route_ref = refs[11:]
    else:
        xo_ref, h_ref = refs[11:]
    merged = None
    for n, (o_ref, g_ref) in enumerate(((osb_ref, g0_ref), (omla_ref, g1_ref), (odiff_ref, g2_ref))):
        y = jax.nn.sigmoid(g_ref[...].astype(F32)) * _dot(o_ref[...], wb_ref[n])
        merged = y if merged is None else merged + y
    mix = _dot(merged.astype(BF16), wo_ref[...])
    xn = x_ref[...] + mod_ref[0, 2:3, :] * mix
    xo_ref[...] = xn
    h = _norm_mod(xn, ng_ref[...], mod_ref[0, 4:5, :], mod_ref[0, 3:4, :])
    h_ref[...] = h.astype(h_ref.dtype)
    if moe:
        tm = h.shape[0]
        lane = lax.broadcasted_iota(jnp.int32, (tm, LANES), 1)
        lane_f = lane.astype(F32)
        logits = jnp.dot(h, wr_ref[...], precision=HIGHEST, preferred_element_type=F32)
        lg = jnp.where(lane < N_EXPERTS, logits, NEG_BIG)
        m1 = jnp.max(lg, axis=-1, keepdims=True)
        i1 = jnp.min(jnp.where(lg == m1, lane_f, float(LANES)), axis=-1, keepdims=True)
        lg2 = jnp.where(lane_f == i1, NEG_BIG, lg)
        m2 = jnp.max(lg2, axis=-1, keepdims=True)
        i2 = jnp.min(jnp.where(lg2 == m2, lane_f, float(LANES)), axis=-1, keepdims=True)
        e = jnp.exp(m2 - m1)
        w1 = 1.0 / (1.0 + e)
        w2 = e / (1.0 + e)
        route_ref[...] = jnp.where(lane == 0, i1, jnp.where(lane == 1, i2, jnp.where(
            lane == 2, w1, jnp.where(lane == 3, w2, 0.0))))


def _merge(o_sb, o_mla, o_diff, proj, x2, mod, wb, wo, ng, wr, seq, moe):
    t, d = x2.shape
    tm = min(512, seq)
    per_b = seq // tm
    gcol = COL_GATE // d
    in_specs = [pl.BlockSpec((tm, BRANCH_WIDTH), lambda i: (i, 0)),
                pl.BlockSpec((tm, BRANCH_WIDTH), lambda i: (i, 0)),
                pl.BlockSpec((tm, BRANCH_WIDTH), lambda i: (i, 0)),
                pl.BlockSpec((tm, d), lambda i: (i, gcol)),
                pl.BlockSpec((tm, d), lambda i: (i, gcol + 1)),
                pl.BlockSpec((tm, d), lambda i: (i, gcol + 2)),
                pl.BlockSpec((tm, d), lambda i: (i, 0)),
                pl.BlockSpec((1, N_ADA, d), lambda i: (i // per_b, 0, 0)),
                pl.BlockSpec((N_BRANCH, BRANCH_WIDTH, d), lambda i: (0, 0, 0)),
                pl.BlockSpec((d, d), lambda i: (0, 0)),
                pl.BlockSpec((1, d), lambda i: (0, 0))]
    args = [o_sb, o_mla, o_diff, proj, proj, proj, x2, mod, wb, wo, ng]
    out_shape = [jax.ShapeDtypeStruct((t, d), F32), jax.ShapeDtypeStruct((t, d), F32 if moe else BF16)]
    out_specs = [pl.BlockSpec((tm, d), lambda i: (i, 0)), pl.BlockSpec((tm, d), lambda i: (i, 0))]
    if moe:
        in_specs.append(pl.BlockSpec((d, LANES), lambda i: (0, 0)))
        args.append(wr)
        out_shape.append(jax.ShapeDtypeStruct((t, LANES), F32))
        out_specs.append(pl.BlockSpec((tm, LANES), lambda i: (i, 0)))
    return pl.pallas_call(
        functools.partial(_merge_kernel, moe=moe),
        out_shape=tuple(out_shape),
        grid=(t // tm,),
        in_specs=in_specs,
        out_specs=tuple(out_specs),
        compiler_params=_cparams("arbitrary"),
        name="merge_moe" if moe else "merge_dense",
    )(*args)


def _swiglu_partial(h, wg, wu, wd):
    g = _dot(h, wg)
    u = _dot(h, wu)
    return _dot((g * jax.nn.sigmoid(g) * u).astype(BF16), wd)


def _ffn_kernel(h_ref, wg_ref, wu_ref, wd_ref, x_ref, mod_ref, o_ref, acc_ref):
    j = pl.program_id(1)
    part = _swiglu_partial(h_ref[...], wg_ref[...], wu_ref[...], wd_ref[...])

    @pl.when(j == 0)
    def _():
        acc_ref[...] = part

    @pl.when(j > 0)
    def _():
        acc_ref[...] += part

    @pl.when(j == pl.num_programs(1) - 1)
    def _():
        o_ref[...] = x_ref[...] + mod_ref[0, 5:6, :] * acc_ref[...]


def _ff_chunk(d_ff):
    return 1408 if d_ff % 1408 == 0 else d_ff


def _ffn(h, wg, wu, wd, x2, mod, seq):
    t, d = x2.shape
    d_ff = wg.shape[1]
    tm = min(1024, seq)
    tf = _ff_chunk(d_ff)
    per_b = seq // tm
    return pl.pallas_call(
        _ffn_kernel,
        out_shape=jax.ShapeDtypeStruct((t, d), F32),
        grid=(t // tm, d_ff // tf),
        in_specs=[pl.BlockSpec((tm, d), lambda i, j: (i, 0)),
                  pl.BlockSpec((d, tf), lambda i, j: (0, j)),
                  pl.BlockSpec((d, tf), lambda i, j: (0, j)),
                  pl.BlockSpec((tf, d), lambda i, j: (j, 0)),
                  pl.BlockSpec((tm, d), lambda i, j: (i, 0)),
                  pl.BlockSpec((1, N_ADA, d), lambda i, j: (i // per_b, 0, 0))],
        out_specs=pl.BlockSpec((tm, d), lambda i, j: (i, 0)),
        scratch_shapes=[pltpu.VMEM((tm, d), F32)],
        compiler_params=_cparams("arbitrary", "arbitrary"),
        name="ffn_dense",
    )(h, wg, wu, wd, x2, mod)


def _row_copy(src_hbm, src_row, dst_ref, dst_row, sem):
    return pltpu.make_async_copy(src_hbm.at[pl.ds(src_row, 1), :], dst_ref.at[pl.ds(dst_row, 1), :], sem)


def _moe_kernel(te_ref, nu_ref, rt_ref, h_hbm, wg_ref, wu_ref, wd_ref, y_ref, xf_ref, xb_ref, acc_ref, sem):
    i = pl.program_id(0)
    j = pl.program_id(1)
    last = pl.num_programs(1) - 1
    tm = xf_ref.shape[0]
    valid = i < nu_ref[0]

    @pl.when(jnp.logical_and(valid, j == 0))
    def _():
        base = i * tm

        def issue(r, carry):
            _row_copy(h_hbm, rt_ref[base + r], xf_ref, r, sem).start()
            return carry

        lax.fori_loop(0, tm, issue, 0)

        def wait(r, carry):
            _row_copy(h_hbm, rt_ref[base + r], xf_ref, r, sem).wait()
            return carry

        lax.fori_loop(0, tm, wait, 0)
        xb_ref[...] = xf_ref[...].astype(BF16)

    @pl.when(valid)
    def _():
        part = _swiglu_partial(xb_ref[...], wg_ref[0], wu_ref[0], wd_ref[0])

        @pl.when(j == 0)
        def _():
            acc_ref[...] = part

        @pl.when(j > 0)
        def _():
            acc_ref[...] += part

        @pl.when(j == last)
        def _():
            y_ref[...] = acc_ref[...]

    @pl.when(jnp.logical_and(jnp.logical_not(valid), j == last))
    def _():
        y_ref[...] = jnp.zeros_like(y_ref)


def _moe_experts(h, wg, wu, wd, tile_expert, n_used, row_token, tm):
    t, d = h.shape
    d_ff = wg.shape[2]
    tf = _ff_chunk(d_ff)
    nj = d_ff // tf
    n_tiles = tile_expert.shape[0]

    def chunk(i, j, nu):
        return jnp.where(i < nu[0], j, nj - 1)

    grid_spec = pltpu.PrefetchScalarGridSpec(
        num_scalar_prefetch=3,
        grid=(n_tiles, nj),
        in_specs=[pl.BlockSpec(memory_space=pl.ANY),
                  pl.BlockSpec((1, d, tf), lambda i, j, te, nu, rt: (te[i], 0, chunk(i, j, nu))),
                  pl.BlockSpec((1, d, tf), lambda i, j, te, nu, rt: (te[i], 0, chunk(i, j, nu))),
                  pl.BlockSpec((1, tf, d), lambda i, j, te, nu, rt: (te[i], chunk(i, j, nu), 0))],
        out_specs=pl.BlockSpec((tm, d), lambda i, j, te, nu, rt: (i, 0)),
        scratch_shapes=[pltpu.VMEM((tm, d), F32), pltpu.VMEM((tm, d), BF16), pltpu.VMEM((tm, d), F32),
                        pltpu.SemaphoreType.DMA],
    )
    return pl.pallas_call(
        _moe_kernel,
        out_shape=jax.ShapeDtypeStruct((n_tiles * tm, d), F32),
        grid_spec=grid_spec,
        compiler_params=_cparams("arbitrary", "arbitrary"),
        name="moe_experts",
    )(tile_expert, n_used, row_token, h, wg, wu, wd)


def _combine_kernel(slot_ref, y_hbm, route_ref, x_ref, mod_ref, o_ref, yb_ref, sem):
    i = pl.program_id(0)
    tm = x_ref.shape[0]
    base = i * tm * TOP_K

    def issue(r, carry):
        for k in range(TOP_K):
            _row_copy(y_hbm, slot_ref[base + r * TOP_K + k], yb_ref.at[k], r, sem).start()
        return carry

    lax.fori_loop(0, tm, issue, 0)

    def wait(r, carry):
        for k in range(TOP_K):
            _row_copy(y_hbm, slot_ref[base + r * TOP_K + k], yb_ref.at[k], r, sem).wait()
        return carry

    lax.fori_loop(0, tm, wait, 0)
    route = route_ref[...]
    f = route[:, 2:3] * yb_ref[0] + route[:, 3:4] * yb_ref[1]
    o_ref[...] = x_ref[...] + mod_ref[0, 5:6, :] * f


def _combine(slot, y, route, x2, mod, seq):
    t, d = x2.shape
    tm = min(256, seq)
    per_b = seq // tm
    grid_spec = pltpu.PrefetchScalarGridSpec(
        num_scalar_prefetch=1,
        grid=(t // tm,),
        in_specs=[pl.BlockSpec(memory_space=pl.ANY),
                  pl.BlockSpec((tm, LANES), lambda i, sl: (i, 0)),
                  pl.BlockSpec((tm, d), lambda i, sl: (i, 0)),
                  pl.BlockSpec((1, N_ADA, d), lambda i, sl: (i // per_b, 0, 0))],
        out_specs=pl.BlockSpec((tm, d), lambda i, sl: (i, 0)),
        scratch_shapes=[pltpu.VMEM((TOP_K, tm, d), F32), pltpu.SemaphoreType.DMA],
    )
    return pl.pallas_call(
        _combine_kernel,
        out_shape=jax.ShapeDtypeStruct((t, d), F32),
        grid_spec=grid_spec,
        compiler_params=_cparams("arbitrary"),
        name="moe_combine",
    )(slot, y, route, x2, mod)


def _routing_tables(route, tm):
    t = route.shape[0]
    p = t * TOP_K
    n_tiles = p // tm + N_EXPERTS
    e = route[:, :TOP_K].astype(jnp.int32).reshape(p)
    onehot = (e[:, None] == jnp.arange(N_EXPERTS, dtype=jnp.int32)[None, :]).astype(jnp.int32)
    csum = jnp.cumsum(onehot, axis=0)
    count = csum[-1]
    rank = jnp.take_along_axis(csum, e[:, None], axis=1)[:, 0] - 1
    tiles_e = (count + tm - 1) // tm
    tile_end = jnp.cumsum(tiles_e)
    tile_start = tile_end - tiles_e
    slot = tile_start[e] * tm + rank
    n_used = tile_end[-1]
    tile_id = jnp.minimum(jnp.arange(n_tiles, dtype=jnp.int32), n_used - 1)
    tile_expert = jnp.sum((tile_id[:, None] >= tile_end[None, :]).astype(jnp.int32), axis=1)
    row_token = jnp.zeros((n_tiles * tm,), jnp.int32).at[slot].set(
        jnp.arange(p, dtype=jnp.int32) // TOP_K, unique_indices=True)
    return tile_expert.astype(jnp.int32), n_used.reshape(1).astype(jnp.int32), row_token, slot.astype(jnp.int32)


def _pack_w_in(w):
    d = w.shape[0]
    sb_cols = 3 * SB_HEADS * HALF
    mla0 = sb_cols
    dif0 = mla0 + MLA_Q_RANK + MLA_KV_RANK + MLA_ROPE
    gate0 = dif0 + 3 * BRANCH_WIDTH
    z = lambda n: jnp.zeros((d, n), w.dtype)
    packed = jnp.concatenate([
        w[:, :sb_cols],
        w[:, mla0:mla0 + MLA_Q_RANK + MLA_KV_RANK],
        w[:, dif0:gate0],
        w[:, gate0:],
        z(HALF), w[:, mla0 + MLA_Q_RANK + MLA_KV_RANK:dif0], z(LANES - HALF - MLA_ROPE)], axis=1)
    assert packed.shape[1] == IN_COLS_PACKED
    return packed.astype(BF16)


def _pack_w_uq(w):
    r = w.shape[0]
    w3 = w.reshape(r, MLA_HEADS, MLA_QK)
    w3 = jnp.concatenate([w3, jnp.zeros((r, MLA_HEADS, LANES - MLA_QK), w.dtype)], axis=2)
    return w3.reshape(r, MLA_HEADS * LANES).astype(BF16)


def _pack_w_ukv(w):
    r = w.shape[0]
    w3 = w.reshape(r, MLA_HEADS, MLA_NOPE + HALF)
    wk = jnp.concatenate([w3[:, :, :MLA_NOPE], jnp.zeros((r, MLA_HEADS, LANES - MLA_NOPE), w.dtype)], axis=2)
    wv = w3[:, :, MLA_NOPE:]
    return wk.reshape(r, MLA_HEADS * LANES).astype(BF16), wv.reshape(r, MLA_HEADS * HALF).astype(BF16)


def _head_lane_gain(g):
    return jnp.concatenate([g, jnp.zeros((LANES - MLA_QK,), g.dtype)])


def _prep_table(mla_q_gain, mla_k_gain, diff_q_gain, diff_k_gain):
    half = MLA_ROPE // 2
    inv_freq = ROPE_THETA ** (-jnp.arange(half, dtype=F32) / half)
    zeros = lambda n: jnp.zeros((n,), F32)
    freq = jnp.concatenate([zeros(MLA_NOPE), inv_freq, inv_freq, zeros(LANES - MLA_QK)])
    sgn_lo = jnp.concatenate([zeros(MLA_NOPE), -jnp.ones((half,), F32), zeros(LANES - MLA_NOPE - half)])
    sgn_hi = jnp.concatenate([zeros(MLA_NOPE + half), jnp.ones((half,), F32), zeros(LANES - MLA_QK)])
    rows = [freq, sgn_lo, sgn_hi,
            _head_lane_gain(mla_q_gain) * (1.0 / math.sqrt(MLA_QK)),
            _head_lane_gain(mla_k_gain),
            jnp.tile(diff_q_gain, 2) * (1.0 / math.sqrt(HALF)),
            jnp.tile(diff_k_gain, 2),
            zeros(LANES)]
    return jnp.stack(rows).astype(F32)


def kernel(x, c, positions, norm1_g, norm2_g, w_ada, b_ada, w_in, mla_q_norm, w_mla_uq, mla_kv_norm,
           w_mla_ukv, mla_q_gain, mla_k_gain, diff_q_gain, diff_k_gain, diff_lambda, diff_subln,
           w_branch, w_out, w_ffn_gate, w_ffn_up, w_ffn_down, w_router, w_exp_gate, w_exp_up, w_exp_down):
    batch, seq, d = x.shape
    depth = w_in.shape[0]
    t = batch * seq
    blk = min(256, seq)
    x2 = x.reshape(t, d)
    pos_col = positions.reshape(t, 1).astype(jnp.int32)
    pos_blk = positions.reshape(batch, seq // blk, 1, blk).astype(jnp.int32)
    slopes = 2.0 ** (-8.0 * jnp.arange(1, DIFF_HEADS + 1, dtype=F32) / DIFF_HEADS)
    mods = _ada(c, w_ada, b_ada).reshape(depth, batch, N_ADA, d)
    moe_tm = min(512, seq)

    for layer in range(depth):
        mod = mods[layer]
        proj = _in_proj(x2, norm1_g[layer].reshape(1, d), mod, _pack_w_in(w_in[layer]), seq)
        wuk, wuv = _pack_w_ukv(w_mla_ukv[layer])
        tab = _prep_table(mla_q_gain[layer], mla_k_gain[layer], diff_q_gain[layer], diff_k_gain[layer])
        qm, km, vm, dqn, dkn = _prep(proj, pos_col, tab, mla_q_norm[layer].reshape(1, -1),
                                     mla_kv_norm[layer].reshape(1, -1), _pack_w_uq(w_mla_uq[layer]),
                                     wuk, wuv, batch, seq)
        o_sb = _sb_attention(proj, batch, seq)
        o_mla = _mla_attention(qm, km, vm, batch, seq)
        lam_init = 0.8 - 0.6 * math.exp(-0.3 * layer)
        o_diff = _diff_attention(dqn, dkn, proj, pos_col, pos_blk, diff_lambda[layer],
                                 diff_subln[layer].reshape(1, LANES), slopes, lam_init, batch, seq)
        j = layer // 2
        moe = layer % 2 == 1
        wr = None
        if moe:
            wr = jnp.concatenate([w_router[j], jnp.zeros((d, LANES - N_EXPERTS), F32)], axis=1)
        outs = _merge(o_sb, o_mla, o_diff, proj, x2, mod, w_branch[layer].astype(BF16),
                      w_out[layer].astype(BF16), norm2_g[layer].reshape(1, d), wr, seq, moe)
        if moe:
            x_mid, h2, route = outs
            tile_expert, n_used, row_token, slot = _routing_tables(route, moe_tm)
            y = _moe_experts(h2, w_exp_gate[j].astype(BF16), w_exp_up[j].astype(BF16),
                             w_exp_down[j].astype(BF16), tile_expert, n_used, row_token, moe_tm)
            x2 = _combine(slot, y, route, x_mid, mod, seq)
        else:
            x_mid, h2 = outs
            x2 = _ffn(h2, w_ffn_gate[j].astype(BF16), w_ffn_up[j].astype(BF16),
                      w_ffn_down[j].astype(BF16), x_mid, mod, seq)
    return x2.reshape(batch, seq, d)
```

```python
import functools
import math

import jax
import jax.numpy as jnp
from jax import lax
from jax.experimental import pallas as pl
from jax.experimental.pallas import tpu as pltpu

F32 = jnp.float32
BF16 = jnp.bfloat16
HIGHEST = lax.Precision.HIGHEST

NORM_EPS = 1e-6
LANES = 128
HALF = 64

SB_HEADS = 8
MLA_HEADS = 8
MLA_Q_RANK = 768
MLA_KV_RANK = 256
MLA_NOPE = 64
MLA_ROPE = 32
MLA_QK = MLA_NOPE + MLA_ROPE
ROPE_THETA = 10000.0
DIFF_HEADS = 4
BRANCH_WIDTH = 512
N_BRANCH = 3
N_EXPERTS = 8
TOP_K = 2
N_ADA = 6
NEG_BIG = -1e30

COL_SB = 0
COL_CQ = 1536
COL_CKV = 2304
COL_DIFF = 2560
COL_GATE = 4096
COL_KROPE = 7168
IN_COLS_PACKED = 7296

VMEM_LIMIT = 56 * 1024 * 1024


def _cparams(*sem):
    return pltpu.CompilerParams(dimension_semantics=sem, vmem_limit_bytes=VMEM_LIMIT)


def _dot(a, b):
    return jnp.dot(a, b, preferred_element_type=F32)


def _dot_nt(a, b):
    return lax.dot_general(a, b, (((1,), (1,)), ((), ())), preferred_element_type=F32)


def _norm_mod(x, g, sc, sh):
    ms = jnp.mean(x * x, axis=-1, keepdims=True)
    return (x * lax.rsqrt(ms + NORM_EPS) * g) * (1.0 + sc) + sh


def _ada_kernel(c_ref, w_ref, b_ref, o_ref):
    c = c_ref[...]
    cond = c * jax.nn.sigmoid(c)
    o_ref[0] = jnp.dot(cond, w_ref[0], precision=HIGHEST, preferred_element_type=F32) + b_ref[0]


def _ada(c, w_ada, b_ada):
    depth, d, n = w_ada.shape
    b = c.shape[0]
    tn = 1536
    return pl.pallas_call(
        _ada_kernel,
        out_shape=jax.ShapeDtypeStruct((depth, b, n), F32),
        grid=(depth, n // tn),
        in_specs=[pl.BlockSpec((b, d), lambda l, j: (0, 0)),
                  pl.BlockSpec((1, d, tn), lambda l, j: (l, 0, j)),
                  pl.BlockSpec((1, 1, tn), lambda l, j: (l, 0, j))],
        out_specs=pl.BlockSpec((1, b, tn), lambda l, j: (l, 0, j)),
        compiler_params=_cparams("arbitrary", "arbitrary"),
        name="ada_mod",
    )(c, w_ada, b_ada.reshape(depth, 1, n))


def _in_proj_kernel(x_ref, g_ref, mod_ref, w_ref, o_ref, h_ref):
    @pl.when(pl.program_id(1) == 0)
    def _():
        h = _norm_mod(x_ref[...], g_ref[...], mod_ref[0, 1:2, :], mod_ref[0, 0:1, :])
        h_ref[...] = h.astype(BF16)

    o_ref[...] = _dot(h_ref[...], w_ref[...]).astype(o_ref.dtype)


def _in_proj(x2, g, mod, w, seq):
    t, d = x2.shape
    n = w.shape[1]
    tm = min(512, seq)
    tn = 2432
    per_b = seq // tm
    return pl.pallas_call(
        _in_proj_kernel,
        out_shape=jax.ShapeDtypeStruct((t, n), BF16),
        grid=(t // tm, n // tn),
        in_specs=[pl.BlockSpec((tm, d), lambda i, j: (i, 0)),
                  pl.BlockSpec((1, d), lambda i, j: (0, 0)),
                  pl.BlockSpec((1, N_ADA, d), lambda i, j: (i // per_b, 0, 0)),
                  pl.BlockSpec((d, tn), lambda i, j: (0, j))],
        out_specs=pl.BlockSpec((tm, tn), lambda i, j: (i, j)),
        scratch_shapes=[pltpu.VMEM((tm, d), BF16)],
        compiler_params=_cparams("arbitrary", "arbitrary"),
        name="in_proj",
    )(x2, g, mod, w)


def _prep_kernel(cq_ref, ckv_ref, kr_ref, dq_ref, dk_ref, pos_ref, tab_ref, qn_ref, kvn_ref,
                 wuq_ref, wuk_ref, wuv_ref, qm_ref, km_ref, vm_ref, dqn_ref, dkn_ref):
    tm = cq_ref.shape[0]
    lane = lax.broadcasted_iota(jnp.int32, (tm, LANES), 1)
    freq = tab_ref[0:1, :]
    sgn_lo = tab_ref[1:2, :]
    sgn_hi = tab_ref[2:3, :]
    gq = tab_ref[3:4, :]
    gk = tab_ref[4:5, :]
    gdq = tab_ref[5:6, :]
    gdk = tab_ref[6:7, :]

    ang = pos_ref[...].astype(F32) * freq
    cos = jnp.cos(ang)
    sin = jnp.sin(ang)
    s_lo = sin * sgn_lo
    s_hi = sin * sgn_hi

    def rope_norm(v, gain):
        r = v * cos + pltpu.roll(v, LANES - MLA_ROPE // 2, 1) * s_lo + pltpu.roll(v, MLA_ROPE // 2, 1) * s_hi
        ms = jnp.sum(r * r, axis=-1, keepdims=True) * (1.0 / MLA_QK)
        return (r * lax.rsqrt(ms + NORM_EPS) * gain).astype(BF16)

    cq = cq_ref[...].astype(F32)
    cqn = cq * lax.rsqrt(jnp.mean(cq * cq, axis=-1, keepdims=True) + NORM_EPS) * qn_ref[...]
    q = _dot(cqn.astype(BF16), wuq_ref[...])
    ckv = ckv_ref[...].astype(F32)
    ckvn = (ckv * lax.rsqrt(jnp.mean(ckv * ckv, axis=-1, keepdims=True) + NORM_EPS) * kvn_ref[...]).astype(BF16)
    kn = _dot(ckvn, wuk_ref[...])
    vm_ref[...] = _dot(ckvn, wuv_ref[...]).astype(BF16)
    kr = kr_ref[...].astype(F32)
    for h in range(MLA_HEADS):
        qm_ref[0, h] = rope_norm(q[:, h * LANES:(h + 1) * LANES], gq)
        km_ref[0, h] = rope_norm(kn[:, h * LANES:(h + 1) * LANES] + kr, gk)

    def half_norm(ref, gain, out_ref):
        for cb in range(ref.shape[1] // LANES):
            v = ref[:, cb * LANES:(cb + 1) * LANES].astype(F32)
            sq = v * v
            lo = jnp.sum(jnp.where(lane < HALF, sq, 0.0), axis=-1, keepdims=True)
            hi = jnp.sum(jnp.where(lane >= HALF, sq, 0.0), axis=-1, keepdims=True)
            ms = jnp.where(lane < HALF, lo, hi) * (1.0 / HALF)
            out_ref[:, cb * LANES:(cb + 1) * LANES] = (v * lax.rsqrt(ms + NORM_EPS) * gain).astype(BF16)

    half_norm(dq_ref, gdq, dqn_ref)
    half_norm(dk_ref, gdk, dkn_ref)


def _prep(proj, pos_col, tab, qn, kvn, wuq, wuk, wuv, batch, seq):
    t = proj.shape[0]
    tm = min(512, seq)
    per_b = seq // tm
    h = MLA_HEADS
    const = lambda shape: pl.BlockSpec(shape, lambda i: (0,) * len(shape))
    return pl.pallas_call(
        _prep_kernel,
        out_shape=(jax.ShapeDtypeStruct((batch, h, seq, LANES), BF16),
                   jax.ShapeDtypeStruct((batch, h, seq, LANES), BF16),
                   jax.ShapeDtypeStruct((t, BRANCH_WIDTH), BF16),
                   jax.ShapeDtypeStruct((t, BRANCH_WIDTH), BF16),
                   jax.ShapeDtypeStruct((t, BRANCH_WIDTH), BF16)),
        grid=(t // tm,),
        in_specs=[pl.BlockSpec((tm, MLA_Q_RANK), lambda i: (i, COL_CQ // MLA_Q_RANK)),
                  pl.BlockSpec((tm, MLA_KV_RANK), lambda i: (i, COL_CKV // MLA_KV_RANK)),
                  pl.BlockSpec((tm, LANES), lambda i: (i, COL_KROPE // LANES)),
                  pl.BlockSpec((tm, BRANCH_WIDTH), lambda i: (i, COL_DIFF // BRANCH_WIDTH)),
                  pl.BlockSpec((tm, BRANCH_WIDTH), lambda i: (i, COL_DIFF // BRANCH_WIDTH + 1)),
                  pl.BlockSpec((tm, 1), lambda i: (i, 0)),
                  const((8, LANES)), const((1, MLA_Q_RANK)), const((1, MLA_KV_RANK)),
                  const((MLA_Q_RANK, h * LANES)), const((MLA_KV_RANK, h * LANES)),
                  const((MLA_KV_RANK, BRANCH_WIDTH))],
        out_specs=(pl.BlockSpec((1, h, tm, LANES), lambda i: (i // per_b, 0, i % per_b, 0)),
                   pl.BlockSpec((1, h, tm, LANES), lambda i: (i // per_b, 0, i % per_b, 0)),
                   pl.BlockSpec((tm, BRANCH_WIDTH), lambda i: (i, 0)),
                   pl.BlockSpec((tm, BRANCH_WIDTH), lambda i: (i, 0)),
                   pl.BlockSpec((tm, BRANCH_WIDTH), lambda i: (i, 0))),
        compiler_params=_cparams("arbitrary"),
        name="attn_prep",
    )(proj, proj, proj, proj, proj, pos_col, tab, qn, kvn, wuq, wuk, wuv)


def _causal_items(nq, backward):
    qb, kb, dg, first = [], [], [], []
    for qi in range(nq):
        order = range(qi, -1, -1) if backward else range(qi + 1)
        for n, j in enumerate(order):
            qb.append(qi), kb.append(j), dg.append(int(j == qi)), first.append(int(n == 0))
    as_arr = lambda v: jnp.asarray(v, dtype=jnp.int32)
    return as_arr(qb), as_arr(kb), as_arr(dg), as_arr(first)


def _run_pipeline(n, phases, carry):
    depth = len(phases)

    def iteration(u, carry, par):
        for d in range(depth):
            carry = phases[d](u - d, (par + d) % 2, carry)
        return carry

    for u in range(depth - 1):
        for d in range(u + 1):
            carry = phases[d](u - d, (u + d) % 2, carry)
    start = depth - 1
    steady = n - start
    if steady % 2 == 1:
        carry = iteration(start, carry, start % 2)
        start += 1
    p0 = start % 2

    def body(t, carry):
        u = start + 2 * t
        carry = iteration(u, carry, p0)
        return iteration(u + 1, carry, 1 - p0)

    carry = lax.fori_loop(0, (n - start) // 2, body, carry)
    for u in range(n, n + depth - 1):
        for d in range(u - n + 1, depth):
            carry = phases[d](u - d, (u + d) % 2, carry)
    return carry


def _first_step():
    return jnp.logical_and(pl.program_id(0) == 0, pl.program_id(1) == 0)


def _sb_kernel(tq_ref, tk_ref, td_ref, tf_ref, q_ref, k_ref, v_ref, o_ref, q_s, acc_s, mmul_s, mbias_s, tri_s,
               *bufs, blk, nq):
    n_items = nq * (nq + 1) // 2
    z_s, hl_s, w_s = bufs[0:4], bufs[4:8], bufs[8:12]
    lane = lax.broadcasted_iota(jnp.int32, (blk, LANES), 1)

    @pl.when(_first_step())
    def _():
        row = lax.broadcasted_iota(jnp.int32, (blk, blk), 0)
        col = lax.broadcasted_iota(jnp.int32, (blk, blk), 1)
        strict = col < row
        mmul_s[0] = jnp.ones((blk, blk), F32)
        mmul_s[1] = jnp.where(strict, 1.0, 0.0)
        mbias_s[0] = jnp.zeros((blk, blk), F32)
        mbias_s[1] = jnp.where(strict, 0.0, NEG_BIG)
        tri = jnp.where(row >= col, 1.0, 0.0).astype(BF16)
        tri_s[0:blk, :] = tri
        tri_s[blk:2 * blk, :] = tri

    scale = math.log2(math.e) / math.sqrt(HALF)
    for qi in range(nq):
        rows = slice(qi * blk, (qi + 1) * blk)
        q = (q_ref[rows, :].astype(F32) * scale).astype(BF16)
        q_s[0, rows, :] = jnp.where(lane < HALF, q, jnp.zeros_like(q))
        q_s[1, rows, :] = jnp.where(lane >= HALF, q, jnp.zeros_like(q))
    acc_s[...] = jnp.zeros(acc_s.shape, F32)

    def rows_of(block):
        return pl.ds(pl.multiple_of(block * blk, blk), blk)

    def scores(k, par, c):
        kb = k_ref[rows_of(tk_ref[k]), :]
        diag = td_ref[k]
        mmul = mmul_s[diag]
        mbias = mbias_s[diag]
        qrows = rows_of(tq_ref[k])
        for s in range(2):
            z = _dot_nt(q_s[s, qrows, :], kb)
            nz = -z
            sp = jnp.log2(1.0 + jnp.exp2(jnp.minimum(z, nz)))
            lneg = (jnp.minimum(nz, 0.0) - sp) * mmul
            hi = lneg.astype(BF16)
            hl_s[2 * par + s][:, 0:blk] = hi
            hl_s[2 * par + s][:, blk:2 * blk] = (lneg - hi.astype(F32)).astype(BF16)
            z_s[2 * par + s][...] = z + mbias
        return c

    def weights(k, par, c):
        diag = td_ref[k]
        out = []
        for s in range(2):
            incl = _dot(hl_s[2 * par + s][...], tri_s[...])
            cs = jnp.where(diag == 1, 0.0, c[s])
            w_s[2 * par + s][...] = jnp.exp2(z_s[2 * par + s][...] + incl + cs).astype(BF16)
            out.append(cs + incl[:, 0:1])
        return tuple(out)

    def values(k, par, c):
        vb = v_ref[rows_of(tk_ref[k]), :]
        qb = tq_ref[k]
        for s in range(2):
            acc_s[qb, s] += _dot(w_s[2 * par + s][...], vb)
        return c

    c = (jnp.zeros((blk, 1), F32), jnp.zeros((blk, 1), F32))
    _run_pipeline(n_items, (scores, weights, values), c)
    for qi in range(nq):
        o_ref[qi * blk:(qi + 1) * blk, :] = jnp.where(lane < HALF, acc_s[qi, 0], acc_s[qi, 1]).astype(o_ref.dtype)


def _sb_attention(proj, batch, seq):
    t = proj.shape[0]
    blk = min(256, seq)
    nq = seq // blk
    npair = SB_HEADS // 2
    qcol = COL_SB // LANES
    kcol = qcol + npair
    vcol = qcol + 2 * npair
    tables = _causal_items(nq, backward=True)
    grid_spec = pltpu.PrefetchScalarGridSpec(
        num_scalar_prefetch=4,
        grid=(batch, npair),
        in_specs=[pl.BlockSpec((seq, LANES), lambda b, p, *_: (b, qcol + p)),
                  pl.BlockSpec((seq, LANES), lambda b, p, *_: (b, kcol + p)),
                  pl.BlockSpec((seq, LANES), lambda b, p, *_: (b, vcol + p))],
        out_specs=pl.BlockSpec((seq, LANES), lambda b, p, *_: (b, p)),
        scratch_shapes=[pltpu.VMEM((2, seq, LANES), BF16),
                        pltpu.VMEM((nq, 2, blk, LANES), F32),
                        pltpu.VMEM((2, blk, blk), F32), pltpu.VMEM((2, blk, blk), F32),
                        pltpu.VMEM((2 * blk, blk), BF16)]
        + [pltpu.VMEM((blk, blk), F32)] * 4
        + [pltpu.VMEM((blk, 2 * blk), BF16)] * 4
        + [pltpu.VMEM((blk, blk), BF16)] * 4,
    )
    return pl.pallas_call(
        functools.partial(_sb_kernel, blk=blk, nq=nq),
        out_shape=jax.ShapeDtypeStruct((t, BRANCH_WIDTH), BF16),
        grid_spec=grid_spec,
        compiler_params=_cparams("arbitrary", "arbitrary"),
        name="sb_attention",
    )(*tables, proj, proj, proj)


def _init_causal_bias(mbias_s, blk):
    row = lax.broadcasted_iota(jnp.int32, (blk, blk), 0)
    col = lax.broadcasted_iota(jnp.int32, (blk, blk), 1)
    mbias_s[0] = jnp.zeros((blk, blk), F32)
    mbias_s[1] = jnp.where(col <= row, 0.0, NEG_BIG)


def _softmax_pipeline(tables, score_fn, v_ref, acc_s, l_s, mbias_s, bufs, *, blk, n_items):
    tq_ref, tk_ref, td_ref, tf_ref = tables
    p_s, a_s = bufs[0:4], bufs[4:8]

    def rows_of(block):
        return pl.ds(pl.multiple_of(block * blk, blk), blk)

    def scores(k, par, carry):
        qb = tq_ref[k]
        first = tf_ref[k] == 1
        mbias = mbias_s[td_ref[k]]
        sc_pair = score_fn(k, rows_of(qb), rows_of(tk_ref[k]))
        out = []
        for s in range(2):
            m_prev = jnp.where(first, NEG_BIG, carry[s][0])
            l_prev = jnp.where(first, 0.0, carry[s][1])
            sc = sc_pair[s] + mbias
            m_new = jnp.maximum(m_prev, jnp.max(sc, axis=-1, keepdims=True))
            alpha = jnp.exp2(m_prev - m_new)
            p = jnp.exp2(sc - m_new)
            l = alpha * l_prev + jnp.sum(p, axis=-1, keepdims=True)
            p_s[2 * par + s][...] = p.astype(BF16)
            a_s[2 * par + s][...] = jnp.broadcast_to(alpha, (blk, LANES))
            l_s[qb, s] = jnp.broadcast_to(l, (blk, LANES))
            out.append((m_new, l))
        return tuple(out)

    def values(k, par, carry):
        vb = v_ref[rows_of(tk_ref[k]), :]
        qb = tq_ref[k]
        for s in range(2):
            acc_s[qb, s] = a_s[2 * par + s][...] * acc_s[qb, s] + _dot(p_s[2 * par + s][...], vb)
        return carry

    init = (jnp.full((blk, 1), NEG_BIG, F32), jnp.zeros((blk, 1), F32))
    _run_pipeline(n_items, (scores, values), (init, init))


def _softmax_scratch(blk, nq):
    return ([pltpu.VMEM((nq, 2, blk, LANES), F32), pltpu.VMEM((nq, 2, blk, LANES), F32),
             pltpu.VMEM((2, blk, blk), F32)]
            + [pltpu.VMEM((blk, blk), BF16)] * 4 + [pltpu.VMEM((blk, LANES), F32)] * 4)


def _mla_kernel(tq_ref, tk_ref, td_ref, tf_ref, q_ref, k_ref, v_ref, o_ref, acc_s, l_s, mbias_s, *bufs, blk, nq):
    lane = lax.broadcasted_iota(jnp.int32, (blk, LANES), 1)

    @pl.when(_first_step())
    def _():
        _init_causal_bias(mbias_s, blk)

    acc_s[...] = jnp.zeros(acc_s.shape, F32)

    def score_fn(k, qrows, krows):
        return tuple(_dot_nt(q_ref[0, s, qrows, :], k_ref[0, s, krows, :]) for s in range(2))

    _softmax_pipeline((tq_ref, tk_ref, td_ref, tf_ref), score_fn, v_ref, acc_s, l_s, mbias_s, bufs,
                      blk=blk, n_items=nq * (nq + 1) // 2)
    for qi in range(nq):
        o0 = acc_s[qi, 0] / l_s[qi, 0]
        o1 = acc_s[qi, 1] / l_s[qi, 1]
        o_ref[qi * blk:(qi + 1) * blk, :] = jnp.where(lane < HALF, o0, o1).astype(o_ref.dtype)


def _mla_attention(qm, km, vm, batch, seq):
    t = vm.shape[0]
    blk = min(256, seq)
    nq = seq // blk
    npair = MLA_HEADS // 2
    grid_spec = pltpu.PrefetchScalarGridSpec(
        num_scalar_prefetch=4,
        grid=(batch, npair),
        in_specs=[pl.BlockSpec((1, 2, seq, LANES), lambda b, p, *_: (b, p, 0, 0)),
                  pl.BlockSpec((1, 2, seq, LANES), lambda b, p, *_: (b, p, 0, 0)),
                  pl.BlockSpec((seq, LANES), lambda b, p, *_: (b, p))],
        out_specs=pl.BlockSpec((seq, LANES), lambda b, p, *_: (b, p)),
        scratch_shapes=_softmax_scratch(blk, nq),
    )
    return pl.pallas_call(
        functools.partial(_mla_kernel, blk=blk, nq=nq),
        out_shape=jax.ShapeDtypeStruct((t, BRANCH_WIDTH), BF16),
        grid_spec=grid_spec,
        compiler_params=_cparams("arbitrary", "arbitrary"),
        name="mla_attention",
    )(*_causal_items(nq, backward=False), qm, km, vm)


def _diff_kernel(tq_ref, tk_ref, td_ref, tf_ref, slope_ref, q_ref, k_ref, v_ref, pq_ref, pk_ref, lam_ref, g_ref,
                 o_ref, q_s, acc_s, l_s, mbias_s, *bufs, blk, nq, lam_init):
    lane = lax.broadcasted_iota(jnp.int32, (blk, LANES), 1)

    @pl.when(_first_step())
    def _():
        _init_causal_bias(mbias_s, blk)

    acc_s[...] = jnp.zeros(acc_s.shape, F32)
    for qi in range(nq):
        rows = slice(qi * blk, (qi + 1) * blk)
        q = q_ref[rows, :]
        q_s[0, rows, :] = jnp.where(lane < HALF, q, jnp.zeros_like(q))
        q_s[1, rows, :] = jnp.where(lane >= HALF, q, jnp.zeros_like(q))
    slope = slope_ref[pl.program_id(1)] * math.log2(math.e)

    def score_fn(k, qrows, krows):
        kb = k_ref[krows, :]
        bias = slope * jnp.abs(pq_ref[qrows, :] - pk_ref[0, tk_ref[k]]).astype(F32)
        return tuple(_dot_nt(q_s[s, qrows, :], kb) - bias for s in range(2))

    _softmax_pipeline((tq_ref, tk_ref, td_ref, tf_ref), score_fn, v_ref, acc_s, l_s, mbias_s, bufs,
                      blk=blk, n_items=nq * (nq + 1) // 2)

    lp = lam_ref[...]
    e1 = jnp.exp(jnp.sum(lp[0:1, :] * lp[1:2, :], axis=-1, keepdims=True))
    e2 = jnp.exp(jnp.sum(lp[2:3, :] * lp[3:4, :], axis=-1, keepdims=True))
    lam = e1 - e2 + lam_init
    for qi in range(nq):
        o = acc_s[qi, 0] / l_s[qi, 0] - lam * (acc_s[qi, 1] / l_s[qi, 1])
        ms = jnp.mean(o * o, axis=-1, keepdims=True)
        o_ref[qi * blk:(qi + 1) * blk, :] = (
            o * lax.rsqrt(ms + NORM_EPS) * g_ref[...] * (1.0 - lam_init)).astype(o_ref.dtype)


def _diff_attention(dqn, dkn, proj, pos_col, pos_blk, diff_lambda, subln, slopes, lam_init, batch, seq):
    t = dqn.shape[0]
    blk = min(256, seq)
    nq = seq // blk
    vcol = (COL_DIFF + 2 * BRANCH_WIDTH) // LANES
    grid_spec = pltpu.PrefetchScalarGridSpec(
        num_scalar_prefetch=5,
        grid=(batch, DIFF_HEADS),
        in_specs=[pl.BlockSpec((seq, LANES), lambda b, h, *_: (b, h)),
                  pl.BlockSpec((seq, LANES), lambda b, h, *_: (b, h)),
                  pl.BlockSpec((seq, LANES), lambda b, h, *_: (b, vcol + h)),
                  pl.BlockSpec((seq, 1), lambda b, h, *_: (b, 0)),
                  pl.BlockSpec((1, nq, 1, blk), lambda b, h, *_: (b, 0, 0, 0)),
                  pl.BlockSpec((4, HALF), lambda b, h, *_: (0, 0)),
                  pl.BlockSpec((1, LANES), lambda b, h, *_: (0, 0))],
        out_specs=pl.BlockSpec((seq, LANES), lambda b, h, *_: (b, h)),
        scratch_shapes=[pltpu.VMEM((2, seq, LANES), BF16)] + _softmax_scratch(blk, nq),
    )
    return pl.pallas_call(
        functools.partial(_diff_kernel, blk=blk, nq=nq, lam_init=lam_init),
        out_shape=jax.ShapeDtypeStruct((t, BRANCH_WIDTH), BF16),
        grid_spec=grid_spec,
        compiler_params=_cparams("arbitrary", "arbitrary"),
        name="diff_attention",
    )(*_causal_items(nq, backward=False), slopes, dqn, dkn, proj, pos_col, pos_blk, diff_lambda, subln)


def _merge_kernel(*refs, moe):
    (osb_ref, omla_ref, odiff_ref, g0_ref, g1_ref, g2_ref, x_ref, mod_ref, wb_ref, wo_ref, ng_ref) = refs[:11]
    if moe:
        wr_ref, xo_ref, h_ref, route_ref = refs[11:]
    else:
        xo_ref, h_ref = refs[11:]
    merged = None
    for n, (o_ref, g_ref) in enumerate(((osb_ref, g0_ref), (omla_ref, g1_ref), (odiff_ref, g2_ref))):
        y = jax.nn.sigmoid(g_ref[...].astype(F32)) * _dot(o_ref[...], wb_ref[n])
        merged = y if merged is None else merged + y
    mix = _dot(merged.astype(BF16), wo_ref[...])
    xn = x_ref[...] + mod_ref[0, 2:3, :] * mix
    xo_ref[...] = xn
    h = _norm_mod(xn, ng_ref[...], mod_ref[0, 4:5, :], mod_ref[0, 3:4, :])
    h_ref[...] = h.astype(h_ref.dtype)
    if moe:
        tm = h.shape[0]
        lane = lax.broadcasted_iota(jnp.int32, (tm, LANES), 1)
        lane_f = lane.astype(F32)
        logits = jnp.dot(h, wr_ref[...], precision=HIGHEST, preferred_element_type=F32)
        lg = jnp.where(lane < N_EXPERTS, logits, NEG_BIG)
        m1 = jnp.max(lg, axis=-1, keepdims=True)
        i1 = jnp.min(jnp.where(lg == m1, lane_f, float(LANES)), axis=-1, keepdims=True)
        lg2 = jnp.where(lane_f == i1, NEG_BIG, lg)
        m2 = jnp.max(lg2, axis=-1, keepdims=True)
        i2 = jnp.min(jnp.where(lg2 == m2, lane_f, float(LANES)), axis=-1, keepdims=True)
        e = jnp.exp(m2 - m1)
        w1 = 1.0 / (1.0 + e)
        w2 = e / (1.0 + e)
        route_ref[...] = jnp.where(lane == 0, i1, jnp.where(lane == 1, i2, jnp.where(
            lane == 2, w1, jnp.where(lane == 3, w2, 0.0))))


def _merge(o_sb, o_mla, o_diff, proj, x2, mod, wb, wo, ng, wr, seq, moe):
    t, d = x2.shape
    tm = min(512, seq)
    per_b = seq // tm
    gcol = COL_GATE // d
    in_specs = [pl.BlockSpec((tm, BRANCH_WIDTH), lambda i: (i, 0)),
                pl.BlockSpec((tm, BRANCH_WIDTH), lambda i: (i, 0)),
                pl.BlockSpec((tm, BRANCH_WIDTH), lambda i: (i, 0)),
                pl.BlockSpec((tm, d), lambda i: (i, gcol)),
                pl.BlockSpec((tm, d), lambda i: (i, gcol + 1)),
                pl.BlockSpec((tm, d), lambda i: (i, gcol + 2)),
                pl.BlockSpec((tm, d), lambda i: (i, 0)),
                pl.BlockSpec((1, N_ADA, d), lambda i: (i // per_b, 0, 0)),
                pl.BlockSpec((N_BRANCH, BRANCH_WIDTH, d), lambda i: (0, 0, 0)),
                pl.BlockSpec((d, d), lambda i: (0, 0)),
                pl.BlockSpec((1, d), lambda i: (0, 0))]
    args = [o_sb, o_mla, o_diff, proj, proj, proj, x2, mod, wb, wo, ng]
    out_shape = [jax.ShapeDtypeStruct((t, d), F32), jax.ShapeDtypeStruct((t, d), F32 if moe else BF16)]
    out_specs = [pl.BlockSpec((tm, d), lambda i: (i, 0)), pl.BlockSpec((tm, d), lambda i: (i, 0))]
    if moe:
        in_specs.append(pl.BlockSpec((d, LANES), lambda i: (0, 0)))
        args.append(wr)
        out_shape.append(jax.ShapeDtypeStruct((t, LANES), F32))
        out_specs.append(pl.BlockSpec((tm, LANES), lambda i: (i, 0)))
    return pl.pallas_call(
        functools.partial(_merge_kernel, moe=moe),
        out_shape=tuple(out_shape),
        grid=(t // tm,),
        in_specs=in_specs,
        out_specs=tuple(out_specs),
        compiler_params=_cparams("arbitrary"),
        name="merge_moe" if moe else "merge_dense",
    )(*args)


def _swiglu_partial(h, wg, wu, wd):
    g = _dot(h, wg)
    u = _dot(h, wu)
    return _dot((g * jax.nn.sigmoid(g) * u).astype(BF16), wd)


def _ffn_kernel(h_ref, wg_ref, wu_ref, wd_ref, x_ref, mod_ref, o_ref, acc_ref):
    j = pl.program_id(1)
    part = _swiglu_partial(h_ref[...], wg_ref[...], wu_ref[...], wd_ref[...])

    @pl.when(j == 0)
    def _():
        acc_ref[...] = part

    @pl.when(j > 0)
    def _():
        acc_ref[...] += part

    @pl.when(j == pl.num_programs(1) - 1)
    def _():
        o_ref[...] = x_ref[...] + mod_ref[0, 5:6, :] * acc_ref[...]


def _ff_chunk(d_ff):
    return 1408 if d_ff % 1408 == 0 else d_ff


def _ffn(h, wg, wu, wd, x2, mod, seq):
    t, d = x2.shape
    d_ff = wg.shape[1]
    tm = min(1024, seq)
    tf = _ff_chunk(d_ff)
    per_b = seq // tm
    return pl.pallas_call(
        _ffn_kernel,
        out_shape=jax.ShapeDtypeStruct((t, d), F32),
        grid=(t // tm, d_ff // tf),
        in_specs=[pl.BlockSpec((tm, d), lambda i, j: (i, 0)),
                  pl.BlockSpec((d, tf), lambda i, j: (0, j)),
                  pl.BlockSpec((d, tf), lambda i, j: (0, j)),
                  pl.BlockSpec((tf, d), lambda i, j: (j, 0)),
                  pl.BlockSpec((tm, d), lambda i, j: (i, 0)),
                  pl.BlockSpec((1, N_ADA, d), lambda i, j: (i // per_b, 0, 0))],
        out_specs=pl.BlockSpec((tm, d), lambda i, j: (i, 0)),
        scratch_shapes=[pltpu.VMEM((tm, d), F32)],
        compiler_params=_cparams("arbitrary", "arbitrary"),
        name="ffn_dense",
    )(h, wg, wu, wd, x2, mod)


def _row_copy(src_hbm, src_row, dst_ref, dst_row, sem):
    return pltpu.make_async_copy(src_hbm.at[pl.ds(src_row, 1), :], dst_ref.at[pl.ds(dst_row, 1), :], sem)


def _moe_kernel(te_ref, nu_ref, rt_ref, h_hbm, wg_ref, wu_ref, wd_ref, y_ref, xf_ref, xb_ref, acc_ref, sem):
    i = pl.program_id(0)
    j = pl.program_id(1)
    last = pl.num_programs(1) - 1
    tm = xf_ref.shape[0]
    valid = i < nu_ref[0]

    @pl.when(jnp.logical_and(valid, j == 0))
    def _():
        base = i * tm

        def issue(r, carry):
            _row_copy(h_hbm, rt_ref[base + r], xf_ref, r, sem).start()
            return carry

        lax.fori_loop(0, tm, issue, 0)

        def wait(r, carry):
            _row_copy(h_hbm, rt_ref[base + r], xf_ref, r, sem).wait()
            return carry

        lax.fori_loop(0, tm, wait, 0)
        xb_ref[...] = xf_ref[...].astype(BF16)

    @pl.when(valid)
    def _():
        part = _swiglu_partial(xb_ref[...], wg_ref[0], wu_ref[0], wd_ref[0])

        @pl.when(j == 0)
        def _():
            acc_ref[...] = part

        @pl.when(j > 0)
        def _():
            acc_ref[...] += part

        @pl.when(j == last)
        def _():
            y_ref[...] = acc_ref[...]

    @pl.when(jnp.logical_and(jnp.logical_not(valid), j == last))
    def _():
        y_ref[...] = jnp.zeros_like(y_ref)


def _moe_experts(h, wg, wu, wd, tile_expert, n_used, row_token, tm):
    t, d = h.shape
    d_ff = wg.shape[2]
    tf = _ff_chunk(d_ff)
    nj = d_ff // tf
    n_tiles = tile_expert.shape[0]

    def chunk(i, j, nu):
        return jnp.where(i < nu[0], j, nj - 1)

    grid_spec = pltpu.PrefetchScalarGridSpec(
        num_scalar_prefetch=3,
        grid=(n_tiles, nj),
        in_specs=[pl.BlockSpec(memory_space=pl.ANY),
                  pl.BlockSpec((1, d, tf), lambda i, j, te, nu, rt: (te[i], 0, chunk(i, j, nu))),
                  pl.BlockSpec((1, d, tf), lambda i, j, te, nu, rt: (te[i], 0, chunk(i, j, nu))),
                  pl.BlockSpec((1, tf, d), lambda i, j, te, nu, rt: (te[i], chunk(i, j, nu), 0))],
        out_specs=pl.BlockSpec((tm, d), lambda i, j, te, nu, rt: (i, 0)),
        scratch_shapes=[pltpu.VMEM((tm, d), F32), pltpu.VMEM((tm, d), BF16), pltpu.VMEM((tm, d), F32),
                        pltpu.SemaphoreType.DMA],
    )
    return pl.pallas_call(
        _moe_kernel,
        out_shape=jax.ShapeDtypeStruct((n_tiles * tm, d), F32),
        grid_spec=grid_spec,
        compiler_params=_cparams("arbitrary", "arbitrary"),
        name="moe_experts",
    )(tile_expert, n_used, row_token, h, wg, wu, wd)


def _combine_kernel(slot_ref, y_hbm, route_ref, x_ref, mod_ref, o_ref, yb_ref, sem):
    i = pl.program_id(0)
    tm = x_ref.shape[0]
    base = i * tm * TOP_K

    def issue(r, carry):
        for k in range(TOP_K):
            _row_copy(y_hbm, slot_ref[base + r * TOP_K + k], yb_ref.at[k], r, sem).start()
        return carry

    lax.fori_loop(0, tm, issue, 0)

    def wait(r, carry):
        for k in range(TOP_K):
            _row_copy(y_hbm, slot_ref[base + r * TOP_K + k], yb_ref.at[k], r, sem).wait()
        return carry

    lax.fori_loop(0, tm, wait, 0)
    route = route_ref[...]
    f = route[:, 2:3] * yb_ref[0] + route[:, 3:4] * yb_ref[1]
    o_ref[...] = x_ref[...] + mod_ref[0, 5:6, :] * f


def _combine(slot, y, route, x2, mod, seq):
    t, d = x2.shape
    tm = min(256, seq)
    per_b = seq // tm
    grid_spec = pltpu.PrefetchScalarGridSpec(
        num_scalar_prefetch=1,
        grid=(t // tm,),
        in_specs=[pl.BlockSpec(memory_space=pl.ANY),
                  pl.BlockSpec((tm, LANES), lambda i, sl: (i, 0)),
                  pl.BlockSpec((tm, d), lambda i, sl: (i, 0)),
                  pl.BlockSpec((1, N_ADA, d), lambda i, sl: (i // per_b, 0, 0))],
        out_specs=pl.BlockSpec((tm, d), lambda i, sl: (i, 0)),
        scratch_shapes=[pltpu.VMEM((TOP_K, tm, d), F32), pltpu.SemaphoreType.DMA],
    )
    return pl.pallas_call(
        _combine_kernel,
        out_shape=jax.ShapeDtypeStruct((t, d), F32),
        grid_spec=grid_spec,
        compiler_params=_cparams("arbitrary"),
        name="moe_combine",
    )(slot, y, route, x2, mod)


def _routing_tables(route, tm):
    t = route.shape[0]
    p = t * TOP_K
    n_tiles = p // tm + N_EXPERTS
    e = route[:, :TOP_K].astype(jnp.int32).reshape(p)
    onehot = (e[:, None] == jnp.arange(N_EXPERTS, dtype=jnp.int32)[None, :]).astype(jnp.int32)
    csum = jnp.cumsum(onehot, axis=0)
    count = csum[-1]
    rank = jnp.take_along_axis(csum, e[:, None], axis=1)[:, 0] - 1
    tiles_e = (count + tm - 1) // tm
    tile_end = jnp.cumsum(tiles_e)
    tile_start = tile_end - tiles_e
    slot = tile_start[e] * tm + rank
    n_used = tile_end[-1]
    tile_id = jnp.minimum(jnp.arange(n_tiles, dtype=jnp.int32), n_used - 1)
    tile_expert = jnp.sum((tile_id[:, None] >= tile_end[None, :]).astype(jnp.int32), axis=1)
    row_token = jnp.zeros((n_tiles * tm,), jnp.int32).at[slot].set(
        jnp.arange(p, dtype=jnp.int32) // TOP_K, unique_indices=True)
    return tile_expert.astype(jnp.int32), n_used.reshape(1).astype(jnp.int32), row_token, slot.astype(jnp.int32)


def _pack_w_in(w):
    d = w.shape[0]
    sb_cols = 3 * SB_HEADS * HALF
    mla0 = sb_cols
    dif0 = mla0 + MLA_Q_RANK + MLA_KV_RANK + MLA_ROPE
    gate0 = dif0 + 3 * BRANCH_WIDTH
    z = lambda n: jnp.zeros((d, n), w.dtype)
    packed = jnp.concatenate([
        w[:, :sb_cols],
        w[:, mla0:mla0 + MLA_Q_RANK + MLA_KV_RANK],
        w[:, dif0:gate0],
        w[:, gate0:],
        z(HALF), w[:, mla0 + MLA_Q_RANK + MLA_KV_RANK:dif0], z(LANES - HALF - MLA_ROPE)], axis=1)
    assert packed.shape[1] == IN_COLS_PACKED
    return packed.astype(BF16)


def _pack_w_uq(w):
    r = w.shape[0]
    w3 = w.reshape(r, MLA_HEADS, MLA_QK)
    w3 = jnp.concatenate([w3, jnp.zeros((r, MLA_HEADS, LANES - MLA_QK), w.dtype)], axis=2)
    return w3.reshape(r, MLA_HEADS * LANES).astype(BF16)


def _pack_w_ukv(w):
    r = w.shape[0]
    w3 = w.reshape(r, MLA_HEADS, MLA_NOPE + HALF)
    wk = jnp.concatenate([w3[:, :, :MLA_NOPE], jnp.zeros((r, MLA_HEADS, LANES - MLA_NOPE), w.dtype)], axis=2)
    wv = w3[:, :, MLA_NOPE:]
    return wk.reshape(r, MLA_HEADS * LANES).astype(BF16), wv.reshape(r, MLA_HEADS * HALF).astype(BF16)


def _head_lane_gain(g):
    return jnp.concatenate([g, jnp.zeros((LANES - MLA_QK,), g.dtype)])


def _prep_table(mla_q_gain, mla_k_gain, diff_q_gain, diff_k_gain):
    half = MLA_ROPE // 2
    inv_freq = ROPE_THETA ** (-jnp.arange(half, dtype=F32) / half)
    zeros = lambda n: jnp.zeros((n,), F32)
    freq = jnp.concatenate([zeros(MLA_NOPE), inv_freq, inv_freq, zeros(LANES - MLA_QK)])
    sgn_lo = jnp.concatenate([zeros(MLA_NOPE), -jnp.ones((half,), F32), zeros(LANES - MLA_NOPE - half)])
    sgn_hi = jnp.concatenate([zeros(MLA_NOPE + half), jnp.ones((half,), F32), zeros(LANES - MLA_QK)])
    log2e = math.log2(math.e)
    rows = [freq, sgn_lo, sgn_hi,
            _head_lane_gain(mla_q_gain) * (log2e / math.sqrt(MLA_QK)),
            _head_lane_gain(mla_k_gain),
            jnp.tile(diff_q_gain, 2) * (log2e / math.sqrt(HALF)),
            jnp.tile(diff_k_gain, 2),
            zeros(LANES)]
    return jnp.stack(rows).astype(F32)


def kernel(x, c, positions, norm1_g, norm2_g, w_ada, b_ada, w_in, mla_q_norm, w_mla_uq, mla_kv_norm,
           w_mla_ukv, mla_q_gain, mla_k_gain, diff_q_gain, diff_k_gain, diff_lambda, diff_subln,
           w_branch, w_out, w_ffn_gate, w_ffn_up, w_ffn_down, w_router, w_exp_gate, w_exp_up, w_exp_down):
    batch, seq, d = x.shape
    depth = w_in.shape[0]
    t = batch * seq
    blk = min(256, seq)
    x2 = x.reshape(t, d)
    pos_col = positions.reshape(t, 1).astype(jnp.int32)
    pos_blk = positions.reshape(batch, seq // blk, 1, blk).astype(jnp.int32)
    slopes = 2.0 ** (-8.0 * jnp.arange(1, DIFF_HEADS + 1, dtype=F32) / DIFF_HEADS)
    mods = _ada(c, w_ada, b_ada).reshape(depth, batch, N_ADA, d)
    moe_tm = min(512, seq)

    for layer in range(depth):
        mod = mods[layer]
        proj = _in_proj(x2, norm1_g[layer].reshape(1, d), mod, _pack_w_in(w_in[layer]), seq)
        wuk, wuv = _pack_w_ukv(w_mla_ukv[layer])
        tab = _prep_table(mla_q_gain[layer], mla_k_gain[layer], diff_q_gain[layer], diff_k_gain[layer])
        qm, km, vm, dqn, dkn = _prep(proj, pos_col, tab, mla_q_norm[layer].reshape(1, -1),
                                     mla_kv_norm[layer].reshape(1, -1), _pack_w_uq(w_mla_uq[layer]),
                                     wuk, wuv, batch, seq)
        o_sb = _sb_attention(proj, batch, seq)
        o_mla = _mla_attention(qm, km, vm, batch, seq)
        lam_init = 0.8 - 0.6 * math.exp(-0.3 * layer)
        o_diff = _diff_attention(dqn, dkn, proj, pos_col, pos_blk, diff_lambda[layer],
                                 diff_subln[layer].reshape(1, LANES), slopes, lam_init, batch, seq)
        j = layer // 2
        moe = layer % 2 == 1
        wr = None
        if moe:
            wr = jnp.concatenate([w_router[j], jnp.zeros((d, LANES - N_EXPERTS), F32)], axis=1)
        outs = _merge(o_sb, o_mla, o_diff, proj, x2, mod, w_branch[layer].astype(BF16),
                      w_out[layer].astype(BF16), norm2_g[layer].reshape(1, d), wr, seq, moe)
        if moe:
            x_mid, h2, route = outs
            tile_expert, n_used, row_token, slot = _routing_tables(route, moe_tm)
            y = _moe_experts(h2, w_exp_gate[j].astype(BF16), w_exp_up[j].astype(BF16),
                             w_exp_down[j].astype(BF16), tile_expert, n_used, row_token, moe_tm)
            x2 = _combine(slot, y, route, x_mid, mod, seq)
        else:
            x_mid, h2 = outs
            x2 = _ffn(h2, w_ffn_gate[j].astype(BF16), w_ffn_up[j].astype(BF16),
                      w_ffn_down[j].astype(BF16), x_mid, mod, seq)
    return x2.reshape(batch, seq, d)
```

```python
import functools
import math

import jax
import jax.numpy as jnp
from jax import lax
from jax.experimental import pallas as pl
from jax.experimental.pallas import tpu as pltpu

F32 = jnp.float32
BF16 = jnp.bfloat16
HIGHEST = lax.Precision.HIGHEST

NORM_EPS = 1e-6
LANES = 128
HALF = 64

SB_HEADS = 8
MLA_HEADS = 8
MLA_Q_RANK = 768
MLA_KV_RANK = 256
MLA_NOPE = 64
MLA_ROPE = 32
MLA_QK = MLA_NOPE + MLA_ROPE
ROPE_THETA = 10000.0
DIFF_HEADS = 4
BRANCH_WIDTH = 512
N_BRANCH = 3
N_EXPERTS = 8
TOP_K = 2
N_ADA = 6
NEG_BIG = -1e30
ONES_ROWS = 16

COL_SB = 0
COL_CQ = 1536
COL_CKV = 2304
COL_DIFF = 2560
COL_GATE = 4096
COL_KROPE = 7168
IN_COLS_PACKED = 7296

VMEM_LIMIT = 56 * 1024 * 1024


def _cparams(*sem):
    return pltpu.CompilerParams(dimension_semantics=sem, vmem_limit_bytes=VMEM_LIMIT)


def _dot(a, b):
    return jnp.dot(a, b, preferred_element_type=F32)


def _dot_nt(a, b):
    return lax.dot_general(a, b, (((1,), (1,)), ((), ())), preferred_element_type=F32)


def _norm_mod(x, g, sc, sh):
    ms = jnp.mean(x * x, axis=-1, keepdims=True)
    return (x * lax.rsqrt(ms + NORM_EPS) * g) * (1.0 + sc) + sh


def _ada_kernel(c_ref, w_ref, b_ref, o_ref):
    c = c_ref[...]
    cond = c * jax.nn.sigmoid(c)
    o_ref[0] = jnp.dot(cond, w_ref[0], precision=HIGHEST, preferred_element_type=F32) + b_ref[0]


def _ada(c, w_ada, b_ada):
    depth, d, n = w_ada.shape
    b = c.shape[0]
    tn = 1536
    return pl.pallas_call(
        _ada_kernel,
        out_shape=jax.ShapeDtypeStruct((depth, b, n), F32),
        grid=(depth, n // tn),
        in_specs=[pl.BlockSpec((b, d), lambda l, j: (0, 0)),
                  pl.BlockSpec((1, d, tn), lambda l, j: (l, 0, j)),
                  pl.BlockSpec((1, 1, tn), lambda l, j: (l, 0, j))],
        out_specs=pl.BlockSpec((1, b, tn), lambda l, j: (l, 0, j)),
        compiler_params=_cparams("arbitrary", "arbitrary"),
        name="ada_mod",
    )(c, w_ada, b_ada.reshape(depth, 1, n))


def _in_proj_kernel(x_ref, g_ref, mod_ref, w_ref, o_ref, h_ref):
    @pl.when(pl.program_id(1) == 0)
    def _():
        h = _norm_mod(x_ref[...], g_ref[...], mod_ref[0, 1:2, :], mod_ref[0, 0:1, :])
        h_ref[...] = h.astype(BF16)

    o_ref[...] = _dot(h_ref[...], w_ref[...]).astype(o_ref.dtype)


def _in_proj(x2, g, mod, w, seq):
    t, d = x2.shape
    n = w.shape[1]
    tm = min(512, seq)
    tn = 2432
    per_b = seq // tm
    return pl.pallas_call(
        _in_proj_kernel,
        out_shape=jax.ShapeDtypeStruct((t, n), BF16),
        grid=(t // tm, n // tn),
        in_specs=[pl.BlockSpec((tm, d), lambda i, j: (i, 0)),
                  pl.BlockSpec((1, d), lambda i, j: (0, 0)),
                  pl.BlockSpec((1, N_ADA, d), lambda i, j: (i // per_b, 0, 0)),
                  pl.BlockSpec((d, tn), lambda i, j: (0, j))],
        out_specs=pl.BlockSpec((tm, tn), lambda i, j: (i, j)),
        scratch_shapes=[pltpu.VMEM((tm, d), BF16)],
        compiler_params=_cparams("arbitrary", "arbitrary"),
        name="in_proj",
    )(x2, g, mod, w)


def _prep_kernel(cq_ref, ckv_ref, kr_ref, dq_ref, dk_ref, pos_ref, tab_ref, qn_ref, kvn_ref,
                 wuq_ref, wuk_ref, wuv_ref, qm_ref, km_ref, vm_ref, dqn_ref, dkn_ref):
    tm = cq_ref.shape[0]
    lane = lax.broadcasted_iota(jnp.int32, (tm, LANES), 1)
    freq = tab_ref[0:1, :]
    sgn_lo = tab_ref[1:2, :]
    sgn_hi = tab_ref[2:3, :]
    gq = tab_ref[3:4, :]
    gk = tab_ref[4:5, :]
    gdq = tab_ref[5:6, :]
    gdk = tab_ref[6:7, :]

    ang = pos_ref[...].astype(F32) * freq
    cos = jnp.cos(ang)
    sin = jnp.sin(ang)
    s_lo = sin * sgn_lo
    s_hi = sin * sgn_hi
    s_pm = s_lo + s_hi

    def qk_norm(r, gain):
        ms = jnp.sum(r * r, axis=-1, keepdims=True) * (1.0 / MLA_QK)
        return (r * lax.rsqrt(ms + NORM_EPS) * gain).astype(BF16)

    cq = cq_ref[...].astype(F32)
    cqn = (cq * lax.rsqrt(jnp.mean(cq * cq, axis=-1, keepdims=True) + NORM_EPS) * qn_ref[...]).astype(BF16)
    q = _dot(cqn, wuq_ref[...])
    ckv = ckv_ref[...].astype(F32)
    ckvn = (ckv * lax.rsqrt(jnp.mean(ckv * ckv, axis=-1, keepdims=True) + NORM_EPS) * kvn_ref[...]).astype(BF16)
    kn = _dot(ckvn, wuk_ref[...])
    vm_ref[...] = _dot(ckvn, wuv_ref[...]).astype(BF16)
    kr = kr_ref[...].astype(F32)
    kr = kr * cos + pltpu.roll(kr, LANES - MLA_ROPE // 2, 1) * s_lo + pltpu.roll(kr, MLA_ROPE // 2, 1) * s_hi
    nh = MLA_HEADS
    for h in range(nh):
        qh = q[:, h * LANES:(h + 1) * LANES] * cos + q[:, (nh + h) * LANES:(nh + h + 1) * LANES] * s_pm
        qm_ref[0, h] = qk_norm(qh, gq)
        km_ref[0, h] = qk_norm(kn[:, h * LANES:(h + 1) * LANES] + kr, gk)

    def half_norm(ref, gain, out_ref):
        for cb in range(ref.shape[1] // LANES):
            v = ref[:, cb * LANES:(cb + 1) * LANES].astype(F32)
            sq = v * v
            lo = jnp.sum(jnp.where(lane < HALF, sq, 0.0), axis=-1, keepdims=True)
            hi = jnp.sum(jnp.where(lane >= HALF, sq, 0.0), axis=-1, keepdims=True)
            ms = jnp.where(lane < HALF, lo, hi) * (1.0 / HALF)
            out_ref[:, cb * LANES:(cb + 1) * LANES] = (v * lax.rsqrt(ms + NORM_EPS) * gain).astype(BF16)

    half_norm(dq_ref, gdq, dqn_ref)
    half_norm(dk_ref, gdk, dkn_ref)


def _prep(proj, pos_col, tab, qn, kvn, wuq, wuk, wuv, batch, seq):
    t = proj.shape[0]
    tm = min(512, seq)
    per_b = seq // tm
    h = MLA_HEADS
    const = lambda shape: pl.BlockSpec(shape, lambda i: (0,) * len(shape))
    return pl.pallas_call(
        _prep_kernel,
        out_shape=(jax.ShapeDtypeStruct((batch, h, seq, LANES), BF16),
                   jax.ShapeDtypeStruct((batch, h, seq, LANES), BF16),
                   jax.ShapeDtypeStruct((t, BRANCH_WIDTH), BF16),
                   jax.ShapeDtypeStruct((t, BRANCH_WIDTH), BF16),
                   jax.ShapeDtypeStruct((t, BRANCH_WIDTH), BF16)),
        grid=(t // tm,),
        in_specs=[pl.BlockSpec((tm, MLA_Q_RANK), lambda i: (i, COL_CQ // MLA_Q_RANK)),
                  pl.BlockSpec((tm, MLA_KV_RANK), lambda i: (i, COL_CKV // MLA_KV_RANK)),
                  pl.BlockSpec((tm, LANES), lambda i: (i, COL_KROPE // LANES)),
                  pl.BlockSpec((tm, BRANCH_WIDTH), lambda i: (i, COL_DIFF // BRANCH_WIDTH)),
                  pl.BlockSpec((tm, BRANCH_WIDTH), lambda i: (i, COL_DIFF // BRANCH_WIDTH + 1)),
                  pl.BlockSpec((tm, 1), lambda i: (i, 0)),
                  const((8, LANES)), const((1, MLA_Q_RANK)), const((1, MLA_KV_RANK)),
                  const((MLA_Q_RANK, 2 * h * LANES)), const((MLA_KV_RANK, h * LANES)),
                  const((MLA_KV_RANK, BRANCH_WIDTH))],
        out_specs=(pl.BlockSpec((1, h, tm, LANES), lambda i: (i // per_b, 0, i % per_b, 0)),
                   pl.BlockSpec((1, h, tm, LANES), lambda i: (i // per_b, 0, i % per_b, 0)),
                   pl.BlockSpec((tm, BRANCH_WIDTH), lambda i: (i, 0)),
                   pl.BlockSpec((tm, BRANCH_WIDTH), lambda i: (i, 0)),
                   pl.BlockSpec((tm, BRANCH_WIDTH), lambda i: (i, 0))),
        compiler_params=_cparams("arbitrary"),
        name="attn_prep",
    )(proj, proj, proj, proj, proj, pos_col, tab, qn, kvn, wuq, wuk, wuv)


def _causal_items(nq, backward):
    qb, kb, dg, first = [], [], [], []
    for qi in range(nq):
        order = range(qi, -1, -1) if backward else range(qi + 1)
        for n, j in enumerate(order):
            qb.append(qi), kb.append(j), dg.append(int(j == qi)), first.append(int(n == 0))
    as_arr = lambda v: jnp.asarray(v, dtype=jnp.int32)
    return as_arr(qb), as_arr(kb), as_arr(dg), as_arr(first)


def _run_pipeline(n, phases, carry, unroll=2):
    depth = len(phases)

    def iteration(u, carry, par):
        for d in range(depth):
            carry = phases[d](u - d, (par + d) % 2, carry)
        return carry

    for u in range(depth - 1):
        for d in range(u + 1):
            carry = phases[d](u - d, (u + d) % 2, carry)
    assert unroll % 2 == 0
    start = depth - 1
    for _ in range((n - start) % unroll):
        carry = iteration(start, carry, start % 2)
        start += 1
    p0 = start % 2

    def body(t, carry):
        u = start + unroll * t
        for r in range(unroll):
            carry = iteration(u + r, carry, (p0 + r) % 2)
        return carry

    carry = lax.fori_loop(0, (n - start) // unroll, body, carry)
    for u in range(n, n + depth - 1):
        for d in range(u - n + 1, depth):
            carry = phases[d](u - d, (u + d) % 2, carry)
    return carry


def _first_step():
    return jnp.logical_and(pl.program_id(0) == 0, pl.program_id(1) == 0)


def _sb_kernel(tq_ref, tk_ref, td_ref, tf_ref, q_ref, k_ref, v_ref, o_ref, q_s, acc_s, mbias_s, tri_s,
               *bufs, blk, nq):
    n_items = nq * (nq + 1) // 2
    z_s, hl_s, w_s = bufs[0:4], bufs[4:8], bufs[8:12]
    lane = lax.broadcasted_iota(jnp.int32, (blk, LANES), 1)

    @pl.when(_first_step())
    def _():
        row = lax.broadcasted_iota(jnp.int32, (blk, blk), 0)
        col = lax.broadcasted_iota(jnp.int32, (blk, blk), 1)
        mbias_s[0] = jnp.zeros((blk, blk), F32)
        mbias_s[1] = jnp.where(col < row, 0.0, NEG_BIG)
        tri = jnp.where(row >= col, 1.0, 0.0).astype(BF16)
        tri_s[0:blk, :] = tri
        tri_s[blk:2 * blk, :] = tri

    scale = math.log2(math.e) / math.sqrt(HALF)
    for qi in range(nq):
        rows = slice(qi * blk, (qi + 1) * blk)
        q = (q_ref[rows, :].astype(F32) * scale).astype(BF16)
        q_s[0, rows, :] = jnp.where(lane < HALF, q, jnp.zeros_like(q))
        q_s[1, rows, :] = jnp.where(lane >= HALF, q, jnp.zeros_like(q))
    acc_s[...] = jnp.zeros(acc_s.shape, F32)

    def rows_of(block):
        return pl.ds(pl.multiple_of(block * blk, blk), blk)

    def scores(k, par, c):
        kb = k_ref[rows_of(tk_ref[k]), :]
        mbias = mbias_s[td_ref[k]]
        qrows = rows_of(tq_ref[k])
        for s in range(2):
            z = _dot_nt(q_s[s, qrows, :], kb) + mbias
            nz = -z
            sp = jnp.log2(1.0 + jnp.exp2(jnp.minimum(z, nz)))
            lneg = jnp.minimum(nz, 0.0) - sp
            hi = lneg.astype(BF16)
            hl_s[2 * par + s][:, 0:blk] = hi
            hl_s[2 * par + s][:, blk:2 * blk] = (lneg - hi.astype(F32)).astype(BF16)
            z_s[2 * par + s][...] = z
        return c

    def weights(k, par, c):
        diag = td_ref[k]
        out = []
        for s in range(2):
            incl = _dot(hl_s[2 * par + s][...], tri_s[...])
            cs = jnp.where(diag == 1, 0.0, c[s])
            w_s[2 * par + s][...] = jnp.exp2(z_s[2 * par + s][...] + incl + cs).astype(BF16)
            out.append(cs + incl[:, 0:1])
        return tuple(out)

    def values(k, par, c):
        vb = v_ref[rows_of(tk_ref[k]), :]
        qb = tq_ref[k]
        for s in range(2):
            acc_s[qb, s] += _dot(w_s[2 * par + s][...], vb)
        return c

    c = (jnp.zeros((blk, 1), F32), jnp.zeros((blk, 1), F32))
    _run_pipeline(n_items, (scores, weights, values), c, unroll=4)
    for qi in range(nq):
        o_ref[qi * blk:(qi + 1) * blk, :] = jnp.where(lane < HALF, acc_s[qi, 0], acc_s[qi, 1]).astype(o_ref.dtype)


def _sb_attention(proj, batch, seq):
    t = proj.shape[0]
    blk = min(256, seq)
    nq = seq // blk
    npair = SB_HEADS // 2
    qcol = COL_SB // LANES
    kcol = qcol + npair
    vcol = qcol + 2 * npair
    tables = _causal_items(nq, backward=True)
    grid_spec = pltpu.PrefetchScalarGridSpec(
        num_scalar_prefetch=4,
        grid=(batch, npair),
        in_specs=[pl.BlockSpec((seq, LANES), lambda b, p, *_: (b, qcol + p)),
                  pl.BlockSpec((seq, LANES), lambda b, p, *_: (b, kcol + p)),
                  pl.BlockSpec((seq, LANES), lambda b, p, *_: (b, vcol + p))],
        out_specs=pl.BlockSpec((seq, LANES), lambda b, p, *_: (b, p)),
        scratch_shapes=[pltpu.VMEM((2, seq, LANES), BF16),
                        pltpu.VMEM((nq, 2, blk, LANES), F32),
                        pltpu.VMEM((2, blk, blk), F32),
                        pltpu.VMEM((2 * blk, blk), BF16)]
        + [pltpu.VMEM((blk, blk), F32)] * 4
        + [pltpu.VMEM((blk, 2 * blk), BF16)] * 4
        + [pltpu.VMEM((blk, blk), BF16)] * 4,
    )
    return pl.pallas_call(
        functools.partial(_sb_kernel, blk=blk, nq=nq),
        out_shape=jax.ShapeDtypeStruct((t, BRANCH_WIDTH), BF16),
        grid_spec=grid_spec,
        compiler_params=_cparams("arbitrary", "arbitrary"),
        name="sb_attention",
    )(*tables, proj, proj, proj)


def _init_causal_bias_t(mbias_s, blk):
    key = lax.broadcasted_iota(jnp.int32, (blk, blk), 0)
    qry = lax.broadcasted_iota(jnp.int32, (blk, blk), 1)
    mbias_s[0] = jnp.zeros((blk, blk), F32)
    mbias_s[1] = jnp.where(key <= qry, 0.0, NEG_BIG)


def _softmax_pipeline(tables, score_fn, vt_fn, acc_s, mbias_s, bufs, *, blk, n_items):
    tq_ref, tk_ref, td_ref, tf_ref = tables
    s_s, p_s, m_s, a_s, a2_s = bufs[0:4], bufs[4:8], bufs[8:12], bufs[12:16], bufs[16:20]

    def rows_of(block):
        return pl.ds(pl.multiple_of(block * blk, blk), blk)

    def scores(k, par, carry):
        first = tf_ref[k] == 1
        mbias = mbias_s[td_ref[k]]
        sc_pair = score_fn(k, rows_of(tq_ref[k]), rows_of(tk_ref[k]))
        m_out = []
        for s in range(2):
            m_prev = jnp.where(first, NEG_BIG, carry[s])
            sc = sc_pair[s] + mbias
            m_new = jnp.maximum(m_prev, jnp.max(sc, axis=0, keepdims=True))
            s_s[2 * par + s][...] = sc
            m_s[2 * par + s][...] = m_new
            a_s[2 * par + s][...] = jnp.exp2(m_prev - m_new)
            m_out.append(m_new)
        return tuple(m_out)

    def probabilities(k, par, carry):
        for s in range(2):
            p_s[2 * par + s][...] = jnp.exp2(s_s[2 * par + s][...] - m_s[2 * par + s][...]).astype(BF16)
            a2_s[2 * par + s][...] = a_s[2 * par + s][...]
        return carry

    def values(k, par, carry):
        kb = tk_ref[k]
        qb = tq_ref[k]
        for s in range(2):
            acc_s[qb, s] = a2_s[2 * par + s][...] * acc_s[qb, s] + _dot(vt_fn(s, kb), p_s[2 * par + s][...])
        return carry

    m0 = jnp.full((1, blk), NEG_BIG, F32)
    _run_pipeline(n_items, (scores, probabilities, values), (m0, m0), unroll=4)


def _softmax_scratch(blk, nq, acc_rows):
    return ([pltpu.VMEM((nq, 2, acc_rows, blk), F32), pltpu.VMEM((2, blk, blk), F32)]
            + [pltpu.VMEM((blk, blk), F32)] * 4 + [pltpu.VMEM((blk, blk), BF16)] * 4
            + [pltpu.VMEM((1, blk), F32)] * 12)


def _mla_kernel(tq_ref, tk_ref, td_ref, tf_ref, q_ref, k_ref, v_ref, o_ref, vt_s, acc_s, mbias_s, *bufs,
                blk, nq):
    vdim = lax.broadcasted_iota(jnp.int32, (LANES, blk), 0)

    @pl.when(_first_step())
    def _():
        _init_causal_bias_t(mbias_s, blk)

    acc_s[...] = jnp.zeros(acc_s.shape, F32)
    for kb in range(nq):
        vt = v_ref[kb * blk:(kb + 1) * blk, :].astype(F32).T
        vt_s[0, kb] = jnp.where(vdim < HALF, vt, 1.0).astype(BF16)
        vt_s[1, kb] = jnp.where(vdim >= HALF, vt, 1.0).astype(BF16)

    def score_fn(k, qrows, krows):
        return tuple(_dot_nt(k_ref[0, s, krows, :], q_ref[0, s, qrows, :]) for s in range(2))

    _softmax_pipeline((tq_ref, tk_ref, td_ref, tf_ref), score_fn, lambda s, kb: vt_s[s, kb], acc_s, mbias_s,
                      bufs, blk=blk, n_items=nq * (nq + 1) // 2)
    for qi in range(nq):
        a0 = acc_s[qi, 0]
        a1 = acc_s[qi, 1]
        o = jnp.where(vdim < HALF, a0 / a0[HALF:HALF + 1, :], a1 / a1[0:1, :])
        o_ref[qi * blk:(qi + 1) * blk, :] = o.T.astype(o_ref.dtype)


def _mla_attention(qm, km, vm, batch, seq):
    t = vm.shape[0]
    blk = min(256, seq)
    nq = seq // blk
    npair = MLA_HEADS // 2
    grid_spec = pltpu.PrefetchScalarGridSpec(
        num_scalar_prefetch=4,
        grid=(batch, npair),
        in_specs=[pl.BlockSpec((1, 2, seq, LANES), lambda b, p, *_: (b, p, 0, 0)),
                  pl.BlockSpec((1, 2, seq, LANES), lambda b, p, *_: (b, p, 0, 0)),
                  pl.BlockSpec((seq, LANES), lambda b, p, *_: (b, p))],
        out_specs=pl.BlockSpec((seq, LANES), lambda b, p, *_: (b, p)),
        scratch_shapes=[pltpu.VMEM((2, nq, LANES, blk), BF16)] + _softmax_scratch(blk, nq, LANES),
    )
    return pl.pallas_call(
        functools.partial(_mla_kernel, blk=blk, nq=nq),
        out_shape=jax.ShapeDtypeStruct((t, BRANCH_WIDTH), BF16),
        grid_spec=grid_spec,
        compiler_params=_cparams("arbitrary", "arbitrary"),
        name="mla_attention",
    )(*_causal_items(nq, backward=False), qm, km, vm)


def _diff_kernel(tq_ref, tk_ref, td_ref, tf_ref, slope_ref, q_ref, k_ref, v_ref, pcol_ref, prow_ref, lam_ref, g_ref,
                 o_ref, q_s, vt_s, pc_s, pr_s, acc_s, mbias_s, *bufs, blk, nq, lam_init):
    lane = lax.broadcasted_iota(jnp.int32, (blk, LANES), 1)

    @pl.when(_first_step())
    def _():
        _init_causal_bias_t(mbias_s, blk)

    acc_s[...] = jnp.zeros(acc_s.shape, F32)
    slope = slope_ref[pl.program_id(1)] * math.log2(math.e)
    pc_s[...] = pcol_ref[...].astype(F32) * slope
    pr_s[...] = prow_ref[0].astype(F32) * slope
    for qi in range(nq):
        rows = slice(qi * blk, (qi + 1) * blk)
        q = q_ref[rows, :]
        q_s[0, rows, :] = jnp.where(lane < HALF, q, jnp.zeros_like(q))
        q_s[1, rows, :] = jnp.where(lane >= HALF, q, jnp.zeros_like(q))
        vt_s[qi, 0:LANES, :] = v_ref[rows, :].astype(F32).T.astype(BF16)
        vt_s[qi, LANES:LANES + ONES_ROWS, :] = jnp.ones((ONES_ROWS, blk), BF16)

    def score_fn(k, qrows, krows):
        kb = k_ref[krows, :]
        bias = jnp.abs(pc_s[krows, :] - pr_s[tq_ref[k]])
        return tuple(_dot_nt(kb, q_s[s, qrows, :]) - bias for s in range(2))

    _softmax_pipeline((tq_ref, tk_ref, td_ref, tf_ref), score_fn, lambda s, kb: vt_s[kb], acc_s, mbias_s,
                      bufs, blk=blk, n_items=nq * (nq + 1) // 2)

    lp = lam_ref[...]
    e1 = jnp.exp(jnp.sum(lp[0:1, :] * lp[1:2, :], axis=-1, keepdims=True))
    e2 = jnp.exp(jnp.sum(lp[2:3, :] * lp[3:4, :], axis=-1, keepdims=True))
    lam = e1 - e2 + lam_init
    for qi in range(nq):
        a0 = acc_s[qi, 0]
        a1 = acc_s[qi, 1]
        o = (a0[0:LANES, :] / a0[LANES:LANES + 1, :] - lam * (a1[0:LANES, :] / a1[LANES:LANES + 1, :])).T
        ms = jnp.mean(o * o, axis=-1, keepdims=True)
        o_ref[qi * blk:(qi + 1) * blk, :] = (
            o * lax.rsqrt(ms + NORM_EPS) * g_ref[...] * (1.0 - lam_init)).astype(o_ref.dtype)


def _diff_attention(dqn, dkn, proj, pos_col, pos_blk, diff_lambda, subln, slopes, lam_init, batch, seq):
    t = dqn.shape[0]
    blk = min(256, seq)
    nq = seq // blk
    vcol = (COL_DIFF + 2 * BRANCH_WIDTH) // LANES
    grid_spec = pltpu.PrefetchScalarGridSpec(
        num_scalar_prefetch=5,
        grid=(batch, DIFF_HEADS),
        in_specs=[pl.BlockSpec((seq, LANES), lambda b, h, *_: (b, h)),
                  pl.BlockSpec((seq, LANES), lambda b, h, *_: (b, h)),
                  pl.BlockSpec((seq, LANES), lambda b, h, *_: (b, vcol + h)),
                  pl.BlockSpec((seq, 1), lambda b, h, *_: (b, 0)),
                  pl.BlockSpec((1, nq, 1, blk), lambda b, h, *_: (b, 0, 0, 0)),
                  pl.BlockSpec((4, HALF), lambda b, h, *_: (0, 0)),
                  pl.BlockSpec((1, LANES), lambda b, h, *_: (0, 0))],
        out_specs=pl.BlockSpec((seq, LANES), lambda b, h, *_: (b, h)),
        scratch_shapes=[pltpu.VMEM((2, seq, LANES), BF16), pltpu.VMEM((nq, LANES + ONES_ROWS, blk), BF16),
                        pltpu.VMEM((seq, 1), F32), pltpu.VMEM((nq, 1, blk), F32)]
        + _softmax_scratch(blk, nq, LANES + ONES_ROWS),
    )
    return pl.pallas_call(
        functools.partial(_diff_kernel, blk=blk, nq=nq, lam_init=lam_init),
        out_shape=jax.ShapeDtypeStruct((t, BRANCH_WIDTH), BF16),
        grid_spec=grid_spec,
        compiler_params=_cparams("arbitrary", "arbitrary"),
        name="diff_attention",
    )(*_causal_items(nq, backward=False), slopes, dqn, dkn, proj, pos_col, pos_blk, diff_lambda, subln)


def _merge_kernel(*refs, moe):
    (osb_ref, omla_ref, odiff_ref, g0_ref, g1_ref, g2_ref, x_ref, mod_ref, wb_ref, wo_ref, ng_ref) = refs[:11]
    if moe:
        wr_ref, xo_ref, h_ref, route_ref = refs[11:]
    else:
        xo_ref, h_ref = refs[11:]
    merged = None
    for n, (o_ref, g_ref) in enumerate(((osb_ref, g0_ref), (omla_ref, g1_ref), (odiff_ref, g2_ref))):
        y = jax.nn.sigmoid(g_ref[...].astype(F32)) * _dot(o_ref[...], wb_ref[n])
        merged = y if merged is None else merged + y
    mix = _dot(merged.astype(BF16), wo_ref[...])
    xn = x_ref[...] + mod_ref[0, 2:3, :] * mix
    xo_ref[...] = xn
    h = _norm_mod(xn, ng_ref[...], mod_ref[0, 4:5, :], mod_ref[0, 3:4, :])
    h_ref[...] = h.astype(h_ref.dtype)
    if moe:
        tm = h.shape[0]
        lane = lax.broadcasted_iota(jnp.int32, (tm, LANES), 1)
        lane_f = lane.astype(F32)
        h_hi = h.astype(BF16)
        h_lo = (h - h_hi.astype(F32)).astype(BF16)
        logits = _dot(h_hi, wr_ref[0]) + _dot(h_lo, wr_ref[0]) + _dot(h_hi, wr_ref[1])
        lg = jnp.where(lane < N_EXPERTS, logits, NEG_BIG)
        m1 = jnp.max(lg, axis=-1, keepdims=True)
        i1 = jnp.min(jnp.where(lg == m1, lane_f, float(LANES)), axis=-1, keepdims=True)
        lg2 = jnp.where(lane_f == i1, NEG_BIG, lg)
        m2 = jnp.max(lg2, axis=-1, keepdims=True)
        i2 = jnp.min(jnp.where(lg2 == m2, lane_f, float(LANES)), axis=-1, keepdims=True)
        e = jnp.exp(m2 - m1)
        w1 = 1.0 / (1.0 + e)
        w2 = e / (1.0 + e)
        route_ref[...] = jnp.where(lane == 0, i1, jnp.where(lane == 1, i2, jnp.where(
            lane == 2, w1, jnp.where(lane == 3, w2, 0.0))))


def _merge(o_sb, o_mla, o_diff, proj, x2, mod, wb, wo, ng, wr, seq, moe):
    t, d = x2.shape
    tm = min(512, seq)
    per_b = seq // tm
    gcol = COL_GATE // d
    in_specs = [pl.BlockSpec((tm, BRANCH_WIDTH), lambda i: (i, 0)),
                pl.BlockSpec((tm, BRANCH_WIDTH), lambda i: (i, 0)),
                pl.BlockSpec((tm, BRANCH_WIDTH), lambda i: (i, 0)),
                pl.BlockSpec((tm, d), lambda i: (i, gcol)),
                pl.BlockSpec((tm, d), lambda i: (i, gcol + 1)),
                pl.BlockSpec((tm, d), lambda i: (i, gcol + 2)),
                pl.BlockSpec((tm, d), lambda i: (i, 0)),
                pl.BlockSpec((1, N_ADA, d), lambda i: (i // per_b, 0, 0)),
                pl.BlockSpec((N_BRANCH, BRANCH_WIDTH, d), lambda i: (0, 0, 0)),
                pl.BlockSpec((d, d), lambda i: (0, 0)),
                pl.BlockSpec((1, d), lambda i: (0, 0))]
    args = [o_sb, o_mla, o_diff, proj, proj, proj, x2, mod, wb, wo, ng]
    out_shape = [jax.ShapeDtypeStruct((t, d), F32), jax.ShapeDtypeStruct((t, d), F32 if moe else BF16)]
    out_specs = [pl.BlockSpec((tm, d), lambda i: (i, 0)), pl.BlockSpec((tm, d), lambda i: (i, 0))]
    if moe:
        in_specs.append(pl.BlockSpec((2, d, LANES), lambda i: (0, 0, 0)))
        args.append(wr)
        out_shape.append(jax.ShapeDtypeStruct((t, LANES), F32))
        out_specs.append(pl.BlockSpec((tm, LANES), lambda i: (i, 0)))
    return pl.pallas_call(
        functools.partial(_merge_kernel, moe=moe),
        out_shape=tuple(out_shape),
        grid=(t // tm,),
        in_specs=in_specs,
        out_specs=tuple(out_specs),
        compiler_params=_cparams("arbitrary"),
        name="merge_moe" if moe else "merge_dense",
    )(*args)


def _swiglu_partial(h, wg, wu, wd):
    g = _dot(h, wg)
    u = _dot(h, wu)
    return _dot((g * jax.nn.sigmoid(g) * u).astype(BF16), wd)


def _ffn_kernel(h_ref, wg_ref, wu_ref, wd_ref, x_ref, mod_ref, o_ref, acc_ref):
    j = pl.program_id(1)
    part = _swiglu_partial(h_ref[...], wg_ref[...], wu_ref[...], wd_ref[...])

    @pl.when(j == 0)
    def _():
        acc_ref[...] = part

    @pl.when(j > 0)
    def _():
        acc_ref[...] += part

    @pl.when(j == pl.num_programs(1) - 1)
    def _():
        o_ref[...] = x_ref[...] + mod_ref[0, 5:6, :] * acc_ref[...]


def _ff_chunk(d_ff):
    return 1408 if d_ff % 1408 == 0 else d_ff


def _ffn(h, wg, wu, wd, x2, mod, seq):
    t, d = x2.shape
    d_ff = wg.shape[1]
    tm = min(1024, seq)
    tf = _ff_chunk(d_ff)
    per_b = seq // tm
    return pl.pallas_call(
        _ffn_kernel,
        out_shape=jax.ShapeDtypeStruct((t, d), F32),
        grid=(t // tm, d_ff // tf),
        in_specs=[pl.BlockSpec((tm, d), lambda i, j: (i, 0)),
                  pl.BlockSpec((d, tf), lambda i, j: (0, j)),
                  pl.BlockSpec((d, tf), lambda i, j: (0, j)),
                  pl.BlockSpec((tf, d), lambda i, j: (j, 0)),
                  pl.BlockSpec((tm, d), lambda i, j: (i, 0)),
                  pl.BlockSpec((1, N_ADA, d), lambda i, j: (i // per_b, 0, 0))],
        out_specs=pl.BlockSpec((tm, d), lambda i, j: (i, 0)),
        scratch_shapes=[pltpu.VMEM((tm, d), F32)],
        compiler_params=_cparams("arbitrary", "arbitrary"),
        name="ffn_dense",
    )(h, wg, wu, wd, x2, mod)


GATHER_UNROLL = 8


def _gather_start(src_hbm, idx_ref, base, dst_ref, sem):
    def group(g, carry):
        r0 = g * GATHER_UNROLL
        for u in range(GATHER_UNROLL):
            row = idx_ref[base + r0 + u]
            pltpu.make_async_copy(src_hbm.at[pl.ds(row, 1), :], dst_ref.at[pl.ds(r0 + u, 1), :], sem).start()
        return carry

    lax.fori_loop(0, dst_ref.shape[0] // GATHER_UNROLL, group, 0)


def _gather_wait(src_hbm, dst_ref, sem):
    pltpu.make_async_copy(src_hbm.at[pl.ds(0, dst_ref.shape[0]), :], dst_ref, sem).wait()


def _moe_kernel(te_ref, nu_ref, rt_ref, h_hbm, wg_ref, wu_ref, wd_ref, y_ref, xf_ref, xb_ref, acc_ref, sem):
    i = pl.program_id(0)
    j = pl.program_id(1)
    last = pl.num_programs(1) - 1
    n_tiles = pl.num_programs(0)
    tm = xb_ref.shape[0]
    valid = i < nu_ref[0]
    slot = i % 2

    @pl.when(j == 0)
    def _():
        @pl.when(i == 0)
        def _():
            _gather_start(h_hbm, rt_ref, 0, xf_ref.at[0], sem.at[0])

        _gather_wait(h_hbm, xf_ref.at[slot], sem.at[slot])
        xb_ref[...] = xf_ref[slot].astype(BF16)

        @pl.when(i + 1 < n_tiles)
        def _():
            _gather_start(h_hbm, rt_ref, (i + 1) * tm, xf_ref.at[1 - slot], sem.at[1 - slot])

    @pl.when(valid)
    def _():
        part = _swiglu_partial(xb_ref[...], wg_ref[0], wu_ref[0], wd_ref[0])

        @pl.when(j == 0)
        def _():
            acc_ref[...] = part

        @pl.when(j > 0)
        def _():
            acc_ref[...] += part

        @pl.when(j == last)
        def _():
            y_ref[...] = acc_ref[...]

    @pl.when(jnp.logical_and(jnp.logical_not(valid), j == last))
    def _():
        y_ref[...] = jnp.zeros_like(y_ref)


def _moe_experts(h, wg, wu, wd, tile_expert, n_used, row_token, tm):
    t, d = h.shape
    d_ff = wg.shape[2]
    tf = _ff_chunk(d_ff)
    nj = d_ff // tf
    n_tiles = tile_expert.shape[0]

    def chunk(i, j, nu):
        return jnp.where(i < nu[0], j, nj - 1)

    grid_spec = pltpu.PrefetchScalarGridSpec(
        num_scalar_prefetch=3,
        grid=(n_tiles, nj),
        in_specs=[pl.BlockSpec(memory_space=pl.ANY),
                  pl.BlockSpec((1, d, tf), lambda i, j, te, nu, rt: (te[i], 0, chunk(i, j, nu))),
                  pl.BlockSpec((1, d, tf), lambda i, j, te, nu, rt: (te[i], 0, chunk(i, j, nu))),
                  pl.BlockSpec((1, tf, d), lambda i, j, te, nu, rt: (te[i], chunk(i, j, nu), 0))],
        out_specs=pl.BlockSpec((tm, d), lambda i, j, te, nu, rt: (i, 0)),
        scratch_shapes=[pltpu.VMEM((2, tm, d), F32), pltpu.VMEM((tm, d), BF16), pltpu.VMEM((tm, d), F32),
                        pltpu.SemaphoreType.DMA((2,))],
    )
    return pl.pallas_call(
        _moe_kernel,
        out_shape=jax.ShapeDtypeStruct((n_tiles * tm, d), F32),
        grid_spec=grid_spec,
        compiler_params=_cparams("arbitrary", "arbitrary"),
        name="moe_experts",
    )(tile_expert, n_used, row_token, h, wg, wu, wd)


def _combine_kernel(slot_ref, y_hbm, route_ref, x_ref, mod_ref, o_ref, yb_ref, sem):
    i = pl.program_id(0)
    n = pl.num_programs(0)
    tm = x_ref.shape[0]
    rows = tm * TOP_K
    buf = i % 2

    @pl.when(i == 0)
    def _():
        _gather_start(y_hbm, slot_ref, 0, yb_ref.at[0], sem.at[0])

    @pl.when(i + 1 < n)
    def _():
        _gather_start(y_hbm, slot_ref, (i + 1) * rows, yb_ref.at[1 - buf], sem.at[1 - buf])

    _gather_wait(y_hbm, yb_ref.at[buf], sem.at[buf])
    route = route_ref[...]
    f = route[:, 2:3] * yb_ref[buf, 0:tm, :] + route[:, 3:4] * yb_ref[buf, tm:rows, :]
    o_ref[...] = x_ref[...] + mod_ref[0, 5:6, :] * f


def _combine(slot, y, route, x2, mod, seq):
    t, d = x2.shape
    tm = min(256, seq)
    per_b = seq // tm
    grid_spec = pltpu.PrefetchScalarGridSpec(
        num_scalar_prefetch=1,
        grid=(t // tm,),
        in_specs=[pl.BlockSpec(memory_space=pl.ANY),
                  pl.BlockSpec((tm, LANES), lambda i, sl: (i, 0)),
                  pl.BlockSpec((tm, d), lambda i, sl: (i, 0)),
                  pl.BlockSpec((1, N_ADA, d), lambda i, sl: (i // per_b, 0, 0))],
        out_specs=pl.BlockSpec((tm, d), lambda i, sl: (i, 0)),
        scratch_shapes=[pltpu.VMEM((2, TOP_K * tm, d), F32), pltpu.SemaphoreType.DMA((2,))],
    )
    slot_tiles = slot.reshape(t // tm, tm, TOP_K).transpose(0, 2, 1).reshape(-1)
    return pl.pallas_call(
        _combine_kernel,
        out_shape=jax.ShapeDtypeStruct((t, d), F32),
        grid_spec=grid_spec,
        compiler_params=_cparams("arbitrary"),
        name="moe_combine",
    )(slot_tiles, y, route, x2, mod)


def _routing_tables(route, tm):
    t = route.shape[0]
    p = t * TOP_K
    n_tiles = p // tm + N_EXPERTS
    e = route[:, :TOP_K].astype(jnp.int32).reshape(p)
    onehot = (e[:, None] == jnp.arange(N_EXPERTS, dtype=jnp.int32)[None, :]).astype(jnp.int32)
    csum = jnp.cumsum(onehot, axis=0)
    count = csum[-1]
    rank = jnp.take_along_axis(csum, e[:, None], axis=1)[:, 0] - 1
    tiles_e = (count + tm - 1) // tm
    tile_end = jnp.cumsum(tiles_e)
    tile_start = tile_end - tiles_e
    slot = tile_start[e] * tm + rank
    n_used = tile_end[-1]
    tile_id = jnp.minimum(jnp.arange(n_tiles, dtype=jnp.int32), n_used - 1)
    tile_expert = jnp.sum((tile_id[:, None] >= tile_end[None, :]).astype(jnp.int32), axis=1)
    row_token = jnp.zeros((n_tiles * tm,), jnp.int32).at[slot].set(
        jnp.arange(p, dtype=jnp.int32) // TOP_K, unique_indices=True)
    return tile_expert.astype(jnp.int32), n_used.reshape(1).astype(jnp.int32), row_token, slot.astype(jnp.int32)


def _pack_w_in(w):
    d = w.shape[0]
    sb_cols = 3 * SB_HEADS * HALF
    mla0 = sb_cols
    dif0 = mla0 + MLA_Q_RANK + MLA_KV_RANK + MLA_ROPE
    gate0 = dif0 + 3 * BRANCH_WIDTH
    z = lambda n: jnp.zeros((d, n), w.dtype)
    packed = jnp.concatenate([
        w[:, :sb_cols],
        w[:, mla0:mla0 + MLA_Q_RANK + MLA_KV_RANK],
        w[:, dif0:gate0],
        w[:, gate0:],
        z(HALF), w[:, mla0 + MLA_Q_RANK + MLA_KV_RANK:dif0], z(LANES - HALF - MLA_ROPE)], axis=1)
    assert packed.shape[1] == IN_COLS_PACKED
    return packed.astype(BF16)


def _pack_w_uq(w):
    r = w.shape[0]
    half = MLA_ROPE // 2
    w3 = w.reshape(r, MLA_HEADS, MLA_QK)
    z = lambda n: jnp.zeros((r, MLA_HEADS, n), w.dtype)
    plain = jnp.concatenate([w3, z(LANES - MLA_QK)], axis=2)
    swapped = jnp.concatenate([z(MLA_NOPE), w3[:, :, MLA_NOPE + half:], w3[:, :, MLA_NOPE:MLA_NOPE + half],
                               z(LANES - MLA_QK)], axis=2)
    return jnp.concatenate([plain.reshape(r, MLA_HEADS * LANES), swapped.reshape(r, MLA_HEADS * LANES)],
                           axis=1).astype(BF16)


def _pack_w_ukv(w):
    r = w.shape[0]
    w3 = w.reshape(r, MLA_HEADS, MLA_NOPE + HALF)
    wk = jnp.concatenate([w3[:, :, :MLA_NOPE], jnp.zeros((r, MLA_HEADS, LANES - MLA_NOPE), w.dtype)], axis=2)
    wv = w3[:, :, MLA_NOPE:]
    return wk.reshape(r, MLA_HEADS * LANES).astype(BF16), wv.reshape(r, MLA_HEADS * HALF).astype(BF16)


def _head_lane_gain(g):
    return jnp.concatenate([g, jnp.zeros((LANES - MLA_QK,), g.dtype)])


def _prep_table(mla_q_gain, mla_k_gain, diff_q_gain, diff_k_gain):
    half = MLA_ROPE // 2
    inv_freq = ROPE_THETA ** (-jnp.arange(half, dtype=F32) / half)
    zeros = lambda n: jnp.zeros((n,), F32)
    freq = jnp.concatenate([zeros(MLA_NOPE), inv_freq, inv_freq, zeros(LANES - MLA_QK)])
    sgn_lo = jnp.concatenate([zeros(MLA_NOPE), -jnp.ones((half,), F32), zeros(LANES - MLA_NOPE - half)])
    sgn_hi = jnp.concatenate([zeros(MLA_NOPE + half), jnp.ones((half,), F32), zeros(LANES - MLA_QK)])
    log2e = math.log2(math.e)
    rows = [freq, sgn_lo, sgn_hi,
            _head_lane_gain(mla_q_gain) * (log2e / math.sqrt(MLA_QK)),
            _head_lane_gain(mla_k_gain),
            jnp.tile(diff_q_gain, 2) * (log2e / math.sqrt(HALF)),
            jnp.tile(diff_k_gain, 2),
            zeros(LANES)]
    return jnp.stack(rows).astype(F32)


def kernel(x, c, positions, norm1_g, norm2_g, w_ada, b_ada, w_in, mla_q_norm, w_mla_uq, mla_kv_norm,
           w_mla_ukv, mla_q_gain, mla_k_gain, diff_q_gain, diff_k_gain, diff_lambda, diff_subln,
           w_branch, w_out, w_ffn_gate, w_ffn_up, w_ffn_down, w_router, w_exp_gate, w_exp_up, w_exp_down):
    batch, seq, d = x.shape
    depth = w_in.shape[0]
    t = batch * seq
    blk = min(256, seq)
    x2 = x.reshape(t, d)
    pos_col = positions.reshape(t, 1).astype(jnp.int32)
    pos_blk = positions.reshape(batch, seq // blk, 1, blk).astype(jnp.int32)
    slopes = 2.0 ** (-8.0 * jnp.arange(1, DIFF_HEADS + 1, dtype=F32) / DIFF_HEADS)
    mods = _ada(c, w_ada, b_ada).reshape(depth, batch, N_ADA, d)
    moe_tm = min(512, seq)

    for layer in range(depth):
        mod = mods[layer]
        proj = _in_proj(x2, norm1_g[layer].reshape(1, d), mod, _pack_w_in(w_in[layer]), seq)
        wuk, wuv = _pack_w_ukv(w_mla_ukv[layer])
        tab = _prep_table(mla_q_gain[layer], mla_k_gain[layer], diff_q_gain[layer], diff_k_gain[layer])
        qm, km, vm, dqn, dkn = _prep(proj, pos_col, tab, mla_q_norm[layer].reshape(1, -1),
                                     mla_kv_norm[layer].reshape(1, -1), _pack_w_uq(w_mla_uq[layer]),
                                     wuk, wuv, batch, seq)
        o_sb = _sb_attention(proj, batch, seq)
        o_mla = _mla_attention(qm, km, vm, batch, seq)
        lam_init = 0.8 - 0.6 * math.exp(-0.3 * layer)
        o_diff = _diff_attention(dqn, dkn, proj, pos_col, pos_blk, diff_lambda[layer],
                                 diff_subln[layer].reshape(1, LANES), slopes, lam_init, batch, seq)
        j = layer // 2
        moe = layer % 2 == 1
        wr = None
        if moe:
            wr = jnp.concatenate([w_router[j], jnp.zeros((d, LANES - N_EXPERTS), F32)], axis=1)
            wr_hi = wr.astype(BF16)
            wr = jnp.stack([wr_hi, (wr - wr_hi.astype(F32)).astype(BF16)])
        outs = _merge(o_sb, o_mla, o_diff, proj, x2, mod, w_branch[layer].astype(BF16),
                      w_out[layer].astype(BF16), norm2_g[layer].reshape(1, d), wr, seq, moe)
        if moe:
            x_mid, h2, route = outs
            tile_expert, n_used, row_token, slot = _routing_tables(route, moe_tm)
            y = _moe_experts(h2, w_exp_gate[j].astype(BF16), w_exp_up[j].astype(BF16),
                             w_exp_down[j].astype(BF16), tile_expert, n_used, row_token, moe_tm)
            x2 = _combine(slot, y, route, x_mid, mod, seq)
        else:
            x_mid, h2 = outs
            x2 = _ffn(h2, w_ffn_gate[j].astype(BF16), w_ffn_up[j].astype(BF16),
                      w_ffn_down[j].astype(BF16), x_mid, mod, seq)
    return x2.reshape(batch, seq, d)
```

```python
import functools
import math

import jax
import jax.numpy as jnp
from jax import lax
from jax.experimental import pallas as pl
from jax.experimental.pallas import tpu as pltpu

F32 = jnp.float32
BF16 = jnp.bfloat16
HIGHEST = lax.Precision.HIGHEST

NORM_EPS = 1e-6
LANES = 128
HALF = 64

SB_HEADS = 8
MLA_HEADS = 8
MLA_Q_RANK = 768
MLA_KV_RANK = 256
MLA_NOPE = 64
MLA_ROPE = 32
MLA_QK = MLA_NOPE + MLA_ROPE
ROPE_THETA = 10000.0
DIFF_HEADS = 4
BRANCH_WIDTH = 512
N_BRANCH = 3
N_EXPERTS = 8
TOP_K = 2
N_ADA = 6
NEG_BIG = -1e30
ONES_ROWS = 16
SB_PIPELINED_BLOCKS = 2
SB_UNDERFLOW_LOG2 = -160.0

COL_SB = 0
COL_CQ = 1536
COL_CKV = 2304
COL_DIFF = 2560
COL_GATE = 4096
COL_KROPE = 7168
IN_COLS_PACKED = 7296

VMEM_LIMIT = 56 * 1024 * 1024


def _cparams(*sem):
    return pltpu.CompilerParams(dimension_semantics=sem, vmem_limit_bytes=VMEM_LIMIT)


def _dot(a, b):
    return jnp.dot(a, b, preferred_element_type=F32)


def _dot_nt(a, b):
    return lax.dot_general(a, b, (((1,), (1,)), ((), ())), preferred_element_type=F32)


def _norm_mod(x, g, sc, sh):
    ms = jnp.mean(x * x, axis=-1, keepdims=True)
    return (x * lax.rsqrt(ms + NORM_EPS) * g) * (1.0 + sc) + sh


def _ada_kernel(c_ref, w_ref, b_ref, o_ref):
    c = c_ref[...]
    cond = c * jax.nn.sigmoid(c)
    o_ref[0] = jnp.dot(cond, w_ref[0], precision=HIGHEST, preferred_element_type=F32) + b_ref[0]


def _ada(c, w_ada, b_ada):
    depth, d, n = w_ada.shape
    b = c.shape[0]
    tn = 1536
    return pl.pallas_call(
        _ada_kernel,
        out_shape=jax.ShapeDtypeStruct((depth, b, n), F32),
        grid=(depth, n // tn),
        in_specs=[pl.BlockSpec((b, d), lambda l, j: (0, 0)),
                  pl.BlockSpec((1, d, tn), lambda l, j: (l, 0, j)),
                  pl.BlockSpec((1, 1, tn), lambda l, j: (l, 0, j))],
        out_specs=pl.BlockSpec((1, b, tn), lambda l, j: (l, 0, j)),
        compiler_params=_cparams("arbitrary", "arbitrary"),
        name="ada_mod",
    )(c, w_ada, b_ada.reshape(depth, 1, n))


def _in_proj_kernel(x_ref, g_ref, mod_ref, w_ref, o_ref, h_ref):
    @pl.when(pl.program_id(1) == 0)
    def _():
        h = _norm_mod(x_ref[...], g_ref[...], mod_ref[0, 1:2, :], mod_ref[0, 0:1, :])
        h_ref[...] = h.astype(BF16)

    o_ref[...] = _dot(h_ref[...], w_ref[...]).astype(o_ref.dtype)


def _in_proj(x2, g, mod, w, seq):
    t, d = x2.shape
    n = w.shape[1]
    tm = min(1024, seq)
    tn = 2432
    per_b = seq // tm
    return pl.pallas_call(
        _in_proj_kernel,
        out_shape=jax.ShapeDtypeStruct((t, n), BF16),
        grid=(t // tm, n // tn),
        in_specs=[pl.BlockSpec((tm, d), lambda i, j: (i, 0)),
                  pl.BlockSpec((1, d), lambda i, j: (0, 0)),
                  pl.BlockSpec((1, N_ADA, d), lambda i, j: (i // per_b, 0, 0)),
                  pl.BlockSpec((d, tn), lambda i, j: (0, j))],
        out_specs=pl.BlockSpec((tm, tn), lambda i, j: (i, j)),
        scratch_shapes=[pltpu.VMEM((tm, d), BF16)],
        compiler_params=_cparams("arbitrary", "arbitrary"),
        name="in_proj",
    )(x2, g, mod, w)


def _prep_kernel(cq_ref, ckv_ref, kr_ref, dq_ref, dk_ref, pos_ref, tab_ref, qn_ref, kvn_ref,
                 wuq_ref, wuk_ref, wuv_ref, qm_ref, km_ref, vm_ref, dqn_ref, dkn_ref):
    tm = cq_ref.shape[0]
    lane = lax.broadcasted_iota(jnp.int32, (tm, LANES), 1)
    freq = tab_ref[0:1, :]
    sgn_lo = tab_ref[1:2, :]
    sgn_hi = tab_ref[2:3, :]
    gq = tab_ref[3:4, :]
    gk = tab_ref[4:5, :]
    gdq = tab_ref[5:6, :]
    gdk = tab_ref[6:7, :]

    ang = pos_ref[...].astype(F32) * freq
    cos = jnp.cos(ang)
    sin = jnp.sin(ang)
    s_lo = sin * sgn_lo
    s_hi = sin * sgn_hi
    s_pm = s_lo + s_hi

    def qk_norm(r, gain):
        ms = jnp.sum(r * r, axis=-1, keepdims=True) * (1.0 / MLA_QK)
        return (r * lax.rsqrt(ms + NORM_EPS) * gain).astype(BF16)

    cq = cq_ref[...].astype(F32)
    cqn = (cq * lax.rsqrt(jnp.mean(cq * cq, axis=-1, keepdims=True) + NORM_EPS) * qn_ref[...]).astype(BF16)
    q = _dot(cqn, wuq_ref[...])
    ckv = ckv_ref[...].astype(F32)
    ckvn = (ckv * lax.rsqrt(jnp.mean(ckv * ckv, axis=-1, keepdims=True) + NORM_EPS) * kvn_ref[...]).astype(BF16)
    kn = _dot(ckvn, wuk_ref[...])
    vm_ref[...] = _dot(ckvn, wuv_ref[...]).astype(BF16)
    kr = kr_ref[...].astype(F32)
    kr = kr * cos + pltpu.roll(kr, LANES - MLA_ROPE // 2, 1) * s_lo + pltpu.roll(kr, MLA_ROPE // 2, 1) * s_hi
    nh = MLA_HEADS
    for h in range(nh):
        qh = q[:, h * LANES:(h + 1) * LANES] * cos + q[:, (nh + h) * LANES:(nh + h + 1) * LANES] * s_pm
        qm_ref[0, h] = qk_norm(qh, gq)
        km_ref[0, h] = qk_norm(kn[:, h * LANES:(h + 1) * LANES] + kr, gk)

    def half_norm(ref, gain, out_ref):
        for cb in range(ref.shape[1] // LANES):
            v = ref[:, cb * LANES:(cb + 1) * LANES].astype(F32)
            sq = v * v
            lo = jnp.sum(jnp.where(lane < HALF, sq, 0.0), axis=-1, keepdims=True)
            hi = jnp.sum(jnp.where(lane >= HALF, sq, 0.0), axis=-1, keepdims=True)
            ms = jnp.where(lane < HALF, lo, hi) * (1.0 / HALF)
            out_ref[:, cb * LANES:(cb + 1) * LANES] = (v * lax.rsqrt(ms + NORM_EPS) * gain).astype(BF16)

    half_norm(dq_ref, gdq, dqn_ref)
    half_norm(dk_ref, gdk, dkn_ref)


def _prep(proj, pos_col, tab, qn, kvn, wuq, wuk, wuv, batch, seq):
    t = proj.shape[0]
    tm = min(512, seq)
    per_b = seq // tm
    h = MLA_HEADS
    const = lambda shape: pl.BlockSpec(shape, lambda i: (0,) * len(shape))
    return pl.pallas_call(
        _prep_kernel,
        out_shape=(jax.ShapeDtypeStruct((batch, h, seq, LANES), BF16),
                   jax.ShapeDtypeStruct((batch, h, seq, LANES), BF16),
                   jax.ShapeDtypeStruct((t, BRANCH_WIDTH), BF16),
                   jax.ShapeDtypeStruct((t, BRANCH_WIDTH), BF16),
                   jax.ShapeDtypeStruct((t, BRANCH_WIDTH), BF16)),
        grid=(t // tm,),
        in_specs=[pl.BlockSpec((tm, MLA_Q_RANK), lambda i: (i, COL_CQ // MLA_Q_RANK)),
                  pl.BlockSpec((tm, MLA_KV_RANK), lambda i: (i, COL_CKV // MLA_KV_RANK)),
                  pl.BlockSpec((tm, LANES), lambda i: (i, COL_KROPE // LANES)),
                  pl.BlockSpec((tm, BRANCH_WIDTH), lambda i: (i, COL_DIFF // BRANCH_WIDTH)),
                  pl.BlockSpec((tm, BRANCH_WIDTH), lambda i: (i, COL_DIFF // BRANCH_WIDTH + 1)),
                  pl.BlockSpec((tm, 1), lambda i: (i, 0)),
                  const((8, LANES)), const((1, MLA_Q_RANK)), const((1, MLA_KV_RANK)),
                  const((MLA_Q_RANK, 2 * h * LANES)), const((MLA_KV_RANK, h * LANES)),
                  const((MLA_KV_RANK, BRANCH_WIDTH))],
        out_specs=(pl.BlockSpec((1, h, tm, LANES), lambda i: (i // per_b, 0, i % per_b, 0)),
                   pl.BlockSpec((1, h, tm, LANES), lambda i: (i // per_b, 0, i % per_b, 0)),
                   pl.BlockSpec((tm, BRANCH_WIDTH), lambda i: (i, 0)),
                   pl.BlockSpec((tm, BRANCH_WIDTH), lambda i: (i, 0)),
                   pl.BlockSpec((tm, BRANCH_WIDTH), lambda i: (i, 0))),
        compiler_params=_cparams("arbitrary"),
        name="attn_prep",
    )(proj, proj, proj, proj, proj, pos_col, tab, qn, kvn, wuq, wuk, wuv)


def _causal_items(nq, backward, max_blocks=None):
    qb, kb, dg, first = [], [], [], []
    for qi in range(nq):
        order = range(qi, -1, -1) if backward else range(qi + 1)
        for n, j in enumerate(list(order)[:max_blocks]):
            qb.append(qi), kb.append(j), dg.append(int(j == qi)), first.append(int(n == 0))
    as_arr = lambda v: jnp.asarray(v, dtype=jnp.int32)
    return as_arr(qb), as_arr(kb), as_arr(dg), as_arr(first)


PIPE_SLOTS = 4


def _run_pipeline(n, phases, carry):
    depth = len(phases)

    def iteration(u, carry, u_mod):
        for d in range(depth):
            carry = phases[d](u - d, (u_mod - d) % PIPE_SLOTS, carry)
        return carry

    for u in range(depth - 1):
        for d in range(u + 1):
            carry = phases[d](u - d, (u - d) % PIPE_SLOTS, carry)
    start = depth - 1
    for _ in range((n - start) % PIPE_SLOTS):
        carry = iteration(start, carry, start % PIPE_SLOTS)
        start += 1

    def body(t, carry):
        u = start + PIPE_SLOTS * t
        for r in range(PIPE_SLOTS):
            carry = iteration(u + r, carry, (start + r) % PIPE_SLOTS)
        return carry

    carry = lax.fori_loop(0, (n - start) // PIPE_SLOTS, body, carry)
    for u in range(n, n + depth - 1):
        for d in range(u - n + 1, depth):
            carry = phases[d](u - d, (u - d) % PIPE_SLOTS, carry)
    return carry


def _first_step():
    return jnp.logical_and(pl.program_id(0) == 0, pl.program_id(1) == 0)


def _sb_kernel(tq_ref, tk_ref, td_ref, tf_ref, q_ref, k_ref, v_ref, o_ref, q_s, acc_s, c_s, mbias_s, tri_s,
               *bufs, blk, nq, n_items):
    z_s, hl_s, w_s = bufs[0:4], bufs[4:8], bufs[8:12]
    lane = lax.broadcasted_iota(jnp.int32, (blk, LANES), 1)

    @pl.when(_first_step())
    def _():
        row = lax.broadcasted_iota(jnp.int32, (blk, blk), 0)
        col = lax.broadcasted_iota(jnp.int32, (blk, blk), 1)
        mbias_s[0] = jnp.zeros((blk, blk), F32)
        mbias_s[1] = jnp.where(col < row, 0.0, NEG_BIG)
        tri = jnp.where(row >= col, 1.0, 0.0).astype(BF16)
        tri_s[0:blk, :] = tri
        tri_s[blk:2 * blk, :] = tri

    scale = math.log2(math.e) / math.sqrt(HALF)
    for qi in range(nq):
        rows = slice(qi * blk, (qi + 1) * blk)
        q = (q_ref[rows, :].astype(F32) * scale).astype(BF16)
        q_s[0, rows, :] = jnp.where(lane < HALF, q, jnp.zeros_like(q))
        q_s[1, rows, :] = jnp.where(lane >= HALF, q, jnp.zeros_like(q))
    acc_s[...] = jnp.zeros(acc_s.shape, F32)

    def rows_of(block):
        return pl.ds(pl.multiple_of(block * blk, blk), blk)

    def scores(k, slot, c):
        par = slot % 2
        kb = k_ref[rows_of(tk_ref[k]), :]
        mbias = mbias_s[td_ref[k]]
        qrows = rows_of(tq_ref[k])
        for s in range(2):
            z = _dot_nt(q_s[s, qrows, :], kb) + mbias
            nz = -z
            sp = jnp.log2(1.0 + jnp.exp2(jnp.minimum(z, nz)))
            lneg = jnp.minimum(nz, 0.0) - sp
            hi = lneg.astype(BF16)
            hl_s[2 * par + s][:, 0:blk] = hi
            hl_s[2 * par + s][:, blk:2 * blk] = (lneg - hi.astype(F32)).astype(BF16)
            z_s[2 * par + s][...] = z
        return c

    def weights(k, slot, c):
        par = slot % 2
        diag = td_ref[k]
        qb = tq_ref[k]
        out = []
        for s in range(2):
            incl = _dot(hl_s[2 * par + s][...], tri_s[...])
            cs = jnp.where(diag == 1, 0.0, c[s])
            w_s[2 * par + s][...] = jnp.exp2(z_s[2 * par + s][...] + incl + cs).astype(BF16)
            c_new = cs + incl[:, 0:1]
            c_s[qb, s] = c_new
            out.append(c_new)
        return tuple(out)

    def values(k, slot, c):
        par = slot % 2
        vb = v_ref[rows_of(tk_ref[k]), :]
        qb = tq_ref[k]
        for s in range(2):
            acc_s[qb, s] += _dot(w_s[2 * par + s][...], vb)
        return c

    c = (jnp.zeros((blk, 1), F32), jnp.zeros((blk, 1), F32))
    _run_pipeline(n_items, (scores, weights, values), c)

    def tail_block(qi, j, c):
        kb = k_ref[rows_of(j), :]
        vb = v_ref[rows_of(j), :]
        qrows = rows_of(qi)
        out = []
        for s in range(2):
            z = _dot_nt(q_s[s, qrows, :], kb)
            nz = -z
            lneg = jnp.minimum(nz, 0.0) - jnp.log2(1.0 + jnp.exp2(jnp.minimum(z, nz)))
            hi = lneg.astype(BF16)
            hl = jnp.concatenate([hi, (lneg - hi.astype(F32)).astype(BF16)], axis=1)
            incl = _dot(hl, tri_s[...])
            acc_s[qi, s] += _dot(jnp.exp2(z + incl + c[s]).astype(BF16), vb)
            out.append(c[s] + incl[:, 0:1])
        return tuple(out)

    def alive(qi):
        return jnp.maximum(jnp.max(c_s[qi, 0]), jnp.max(c_s[qi, 1])) > SB_UNDERFLOW_LOG2

    def tail(qi, carry):
        def step(t, carry):
            @pl.when(alive(qi))
            def _():
                c_new = tail_block(qi, qi - SB_PIPELINED_BLOCKS - t, (c_s[qi, 0], c_s[qi, 1]))
                c_s[qi, 0] = c_new[0]
                c_s[qi, 1] = c_new[1]

            return carry

        @pl.when(alive(qi))
        def _():
            lax.fori_loop(0, qi - SB_PIPELINED_BLOCKS + 1, step, 0)

        return carry

    lax.fori_loop(SB_PIPELINED_BLOCKS, nq, tail, 0)
    for qi in range(nq):
        o_ref[qi * blk:(qi + 1) * blk, :] = jnp.where(lane < HALF, acc_s[qi, 0], acc_s[qi, 1]).astype(o_ref.dtype)


def _sb_attention(proj, batch, seq):
    t = proj.shape[0]
    blk = min(256, seq)
    nq = seq // blk
    npair = SB_HEADS // 2
    qcol = COL_SB // LANES
    kcol = qcol + npair
    vcol = qcol + 2 * npair
    tables = _causal_items(nq, backward=True, max_blocks=SB_PIPELINED_BLOCKS)
    grid_spec = pltpu.PrefetchScalarGridSpec(
        num_scalar_prefetch=4,
        grid=(batch, npair),
        in_specs=[pl.BlockSpec((seq, LANES), lambda b, p, *_: (b, qcol + p)),
                  pl.BlockSpec((seq, LANES), lambda b, p, *_: (b, kcol + p)),
                  pl.BlockSpec((seq, LANES), lambda b, p, *_: (b, vcol + p))],
        out_specs=pl.BlockSpec((seq, LANES), lambda b, p, *_: (b, p)),
        scratch_shapes=[pltpu.VMEM((2, seq, LANES), BF16),
                        pltpu.VMEM((nq, 2, blk, LANES), F32),
                        pltpu.VMEM((nq, 2, blk, 1), F32),
                        pltpu.VMEM((2, blk, blk), F32),
                        pltpu.VMEM((2 * blk, blk), BF16)]
        + [pltpu.VMEM((blk, blk), F32)] * 4
        + [pltpu.VMEM((blk, 2 * blk), BF16)] * 4
        + [pltpu.VMEM((blk, blk), BF16)] * 4,
    )
    return pl.pallas_call(
        functools.partial(_sb_kernel, blk=blk, nq=nq, n_items=int(tables[0].shape[0])),
        out_shape=jax.ShapeDtypeStruct((t, BRANCH_WIDTH), BF16),
        grid_spec=grid_spec,
        compiler_params=_cparams("arbitrary", "arbitrary"),
        name="sb_attention",
    )(*tables, proj, proj, proj)


def _init_causal_bias_t(mbias_s, blk):
    key = lax.broadcasted_iota(jnp.int32, (blk, blk), 0)
    qry = lax.broadcasted_iota(jnp.int32, (blk, blk), 1)
    mbias_s[0] = jnp.zeros((blk, blk), F32)
    mbias_s[1] = jnp.where(key <= qry, 0.0, NEG_BIG)


def _softmax_pipeline(tables, score_fn, vt_fn, acc_s, mbias_s, bufs, *, blk, n_items):
    tq_ref, tk_ref, td_ref, tf_ref = tables
    s_s, p_s, m_s, a_s, a2_s = bufs[0:4], bufs[4:8], bufs[8:12], bufs[12:16], bufs[16:20]

    def rows_of(block):
        return pl.ds(pl.multiple_of(block * blk, blk), blk)

    def scores(k, slot, carry):
        par = slot % 2
        first = tf_ref[k] == 1
        mbias = mbias_s[td_ref[k]]
        sc_pair = score_fn(k, rows_of(tq_ref[k]), rows_of(tk_ref[k]))
        m_out = []
        for s in range(2):
            m_prev = jnp.where(first, NEG_BIG, carry[s])
            sc = sc_pair[s] + mbias
            m_new = jnp.maximum(m_prev, jnp.max(sc, axis=0, keepdims=True))
            s_s[2 * par + s][...] = sc
            m_s[2 * par + s][...] = m_new
            a_s[2 * par + s][...] = jnp.exp2(m_prev - m_new)
            m_out.append(m_new)
        return tuple(m_out)

    def probabilities(k, slot, carry):
        par = slot % 2
        for s in range(2):
            p_s[2 * par + s][...] = jnp.exp2(s_s[2 * par + s][...] - m_s[2 * par + s][...]).astype(BF16)
            a2_s[2 * par + s][...] = a_s[2 * par + s][...]
        return carry

    def values(k, slot, carry):
        par = slot % 2
        kb = tk_ref[k]
        qb = tq_ref[k]
        for s in range(2):
            acc_s[qb, s] = a2_s[2 * par + s][...] * acc_s[qb, s] + _dot(vt_fn(s, kb), p_s[2 * par + s][...])
        return carry

    m0 = jnp.full((1, blk), NEG_BIG, F32)
    _run_pipeline(n_items, (scores, probabilities, values), (m0, m0))


def _softmax_scratch(blk, nq, acc_rows):
    return ([pltpu.VMEM((nq, 2, acc_rows, blk), F32), pltpu.VMEM((2, blk, blk), F32)]
            + [pltpu.VMEM((blk, blk), F32)] * 4 + [pltpu.VMEM((blk, blk), BF16)] * 4
            + [pltpu.VMEM((1, blk), F32)] * 12)


def _mla_kernel(tq_ref, tk_ref, td_ref, tf_ref, q_ref, k_ref, v_ref, o_ref, vt_s, acc_s, mbias_s, *bufs,
                blk, nq):
    vdim = lax.broadcasted_iota(jnp.int32, (LANES, blk), 0)

    @pl.when(_first_step())
    def _():
        _init_causal_bias_t(mbias_s, blk)

    acc_s[...] = jnp.zeros(acc_s.shape, F32)
    for kb in range(nq):
        vt = v_ref[kb * blk:(kb + 1) * blk, :].astype(F32).T
        vt_s[0, kb] = jnp.where(vdim < HALF, vt, 1.0).astype(BF16)
        vt_s[1, kb] = jnp.where(vdim >= HALF, vt, 1.0).astype(BF16)

    def score_fn(k, qrows, krows):
        return tuple(_dot_nt(k_ref[0, s, krows, :], q_ref[0, s, qrows, :]) for s in range(2))

    _softmax_pipeline((tq_ref, tk_ref, td_ref, tf_ref), score_fn, lambda s, kb: vt_s[s, kb], acc_s, mbias_s,
                      bufs, blk=blk, n_items=nq * (nq + 1) // 2)
    for qi in range(nq):
        a0 = acc_s[qi, 0]
        a1 = acc_s[qi, 1]
        o = jnp.where(vdim < HALF, a0 / a0[HALF:HALF + 1, :], a1 / a1[0:1, :])
        o_ref[qi * blk:(qi + 1) * blk, :] = o.T.astype(o_ref.dtype)


def _mla_attention(qm, km, vm, batch, seq):
    t = vm.shape[0]
    blk = min(256, seq)
    nq = seq // blk
    npair = MLA_HEADS // 2
    grid_spec = pltpu.PrefetchScalarGridSpec(
        num_scalar_prefetch=4,
        grid=(batch, npair),
        in_specs=[pl.BlockSpec((1, 2, seq, LANES), lambda b, p, *_: (b, p, 0, 0)),
                  pl.BlockSpec((1, 2, seq, LANES), lambda b, p, *_: (b, p, 0, 0)),
                  pl.BlockSpec((seq, LANES), lambda b, p, *_: (b, p))],
        out_specs=pl.BlockSpec((seq, LANES), lambda b, p, *_: (b, p)),
        scratch_shapes=[pltpu.VMEM((2, nq, LANES, blk), BF16)] + _softmax_scratch(blk, nq, LANES),
    )
    return pl.pallas_call(
        functools.partial(_mla_kernel, blk=blk, nq=nq),
        out_shape=jax.ShapeDtypeStruct((t, BRANCH_WIDTH), BF16),
        grid_spec=grid_spec,
        compiler_params=_cparams("arbitrary", "arbitrary"),
        name="mla_attention",
    )(*_causal_items(nq, backward=False), qm, km, vm)


def _diff_kernel(tq_ref, tk_ref, td_ref, tf_ref, slope_ref, q_ref, k_ref, v_ref, pcol_ref, prow_ref, lam_ref, g_ref,
                 o_ref, q_s, vt_s, pc_s, pr_s, acc_s, mbias_s, *bufs, blk, nq, lam_init):
    lane = lax.broadcasted_iota(jnp.int32, (blk, LANES), 1)

    @pl.when(_first_step())
    def _():
        _init_causal_bias_t(mbias_s, blk)

    acc_s[...] = jnp.zeros(acc_s.shape, F32)
    slope = slope_ref[pl.program_id(1)] * math.log2(math.e)
    pc_s[...] = pcol_ref[...].astype(F32) * slope
    pr_s[...] = prow_ref[0].astype(F32) * slope
    for qi in range(nq):
        rows = slice(qi * blk, (qi + 1) * blk)
        q = q_ref[rows, :]
        q_s[0, rows, :] = jnp.where(lane < HALF, q, jnp.zeros_like(q))
        q_s[1, rows, :] = jnp.where(lane >= HALF, q, jnp.zeros_like(q))
        vt_s[qi, 0:LANES, :] = v_ref[rows, :].astype(F32).T.astype(BF16)
        vt_s[qi, LANES:LANES + ONES_ROWS, :] = jnp.ones((ONES_ROWS, blk), BF16)

    def score_fn(k, qrows, krows):
        kb = k_ref[krows, :]
        bias = jnp.abs(pc_s[krows, :] - pr_s[tq_ref[k]])
        return tuple(_dot_nt(kb, q_s[s, qrows, :]) - bias for s in range(2))

    _softmax_pipeline((tq_ref, tk_ref, td_ref, tf_ref), score_fn, lambda s, kb: vt_s[kb], acc_s, mbias_s,
                      bufs, blk=blk, n_items=nq * (nq + 1) // 2)

    lp = lam_ref[...]
    e1 = jnp.exp(jnp.sum(lp[0:1, :] * lp[1:2, :], axis=-1, keepdims=True))
    e2 = jnp.exp(jnp.sum(lp[2:3, :] * lp[3:4, :], axis=-1, keepdims=True))
    lam = e1 - e2 + lam_init
    for qi in range(nq):
        a0 = acc_s[qi, 0]
        a1 = acc_s[qi, 1]
        o = (a0[0:LANES, :] / a0[LANES:LANES + 1, :] - lam * (a1[0:LANES, :] / a1[LANES:LANES + 1, :])).T
        ms = jnp.mean(o * o, axis=-1, keepdims=True)
        o_ref[qi * blk:(qi + 1) * blk, :] = (
            o * lax.rsqrt(ms + NORM_EPS) * g_ref[...] * (1.0 - lam_init)).astype(o_ref.dtype)


def _diff_attention(dqn, dkn, proj, pos_col, pos_blk, diff_lambda, subln, slopes, lam_init, batch, seq):
    t = dqn.shape[0]
    blk = min(256, seq)
    nq = seq // blk
    vcol = (COL_DIFF + 2 * BRANCH_WIDTH) // LANES
    grid_spec = pltpu.PrefetchScalarGridSpec(
        num_scalar_prefetch=5,
        grid=(batch, DIFF_HEADS),
        in_specs=[pl.BlockSpec((seq, LANES), lambda b, h, *_: (b, h)),
                  pl.BlockSpec((seq, LANES), lambda b, h, *_: (b, h)),
                  pl.BlockSpec((seq, LANES), lambda b, h, *_: (b, vcol + h)),
                  pl.BlockSpec((seq, 1), lambda b, h, *_: (b, 0)),
                  pl.BlockSpec((1, nq, 1, blk), lambda b, h, *_: (b, 0, 0, 0)),
                  pl.BlockSpec((4, HALF), lambda b, h, *_: (0, 0)),
                  pl.BlockSpec((1, LANES), lambda b, h, *_: (0, 0))],
        out_specs=pl.BlockSpec((seq, LANES), lambda b, h, *_: (b, h)),
        scratch_shapes=[pltpu.VMEM((2, seq, LANES), BF16), pltpu.VMEM((nq, LANES + ONES_ROWS, blk), BF16),
                        pltpu.VMEM((seq, 1), F32), pltpu.VMEM((nq, 1, blk), F32)]
        + _softmax_scratch(blk, nq, LANES + ONES_ROWS),
    )
    return pl.pallas_call(
        functools.partial(_diff_kernel, blk=blk, nq=nq, lam_init=lam_init),
        out_shape=jax.ShapeDtypeStruct((t, BRANCH_WIDTH), BF16),
        grid_spec=grid_spec,
        compiler_params=_cparams("arbitrary", "arbitrary"),
        name="diff_attention",
    )(*_causal_items(nq, backward=False), slopes, dqn, dkn, proj, pos_col, pos_blk, diff_lambda, subln)


def _merge_kernel(*refs, moe):
    (osb_ref, omla_ref, odiff_ref, g0_ref, g1_ref, g2_ref, x_ref, mod_ref, wb_ref, wo_ref, ng_ref) = refs[:11]
    if moe:
        wr_ref, xo_ref, h_ref, route_ref = refs[11:]
    else:
        xo_ref, h_ref = refs[11:]
    merged = None
    for n, (o_ref, g_ref) in enumerate(((osb_ref, g0_ref), (omla_ref, g1_ref), (odiff_ref, g2_ref))):
        y = jax.nn.sigmoid(g_ref[...].astype(F32)) * _dot(o_ref[...], wb_ref[n])
        merged = y if merged is None else merged + y
    mix = _dot(merged.astype(BF16), wo_ref[...])
    xn = x_ref[...] + mod_ref[0, 2:3, :] * mix
    xo_ref[...] = xn
    h = _norm_mod(xn, ng_ref[...], mod_ref[0, 4:5, :], mod_ref[0, 3:4, :])
    h_ref[...] = h.astype(h_ref.dtype)
    if moe:
        tm = h.shape[0]
        lane = lax.broadcasted_iota(jnp.int32, (tm, LANES), 1)
        lane_f = lane.astype(F32)
        h_hi = h.astype(BF16)
        h_lo = (h - h_hi.astype(F32)).astype(BF16)
        both = _dot(h_hi, wr_ref[...])
        logits = both[:, 0:LANES] + both[:, LANES:2 * LANES] + _dot(h_lo, wr_ref[:, 0:LANES])
        lg = jnp.where(lane < N_EXPERTS, logits, NEG_BIG)
        m1 = jnp.max(lg, axis=-1, keepdims=True)
        i1 = jnp.min(jnp.where(lg == m1, lane_f, float(LANES)), axis=-1, keepdims=True)
        lg2 = jnp.where(lane_f == i1, NEG_BIG, lg)
        m2 = jnp.max(lg2, axis=-1, keepdims=True)
        i2 = jnp.min(jnp.where(lg2 == m2, lane_f, float(LANES)), axis=-1, keepdims=True)
        e = jnp.exp(m2 - m1)
        w1 = 1.0 / (1.0 + e)
        w2 = e / (1.0 + e)
        route_ref[...] = jnp.where(lane == 0, i1, jnp.where(lane == 1, i2, jnp.where(
            lane == 2, w1, jnp.where(lane == 3, w2, 0.0))))


def _merge(o_sb, o_mla, o_diff, proj, x2, mod, wb, wo, ng, wr, seq, moe):
    t, d = x2.shape
    tm = min(512, seq)
    per_b = seq // tm
    gcol = COL_GATE // d
    in_specs = [pl.BlockSpec((tm, BRANCH_WIDTH), lambda i: (i, 0)),
                pl.BlockSpec((tm, BRANCH_WIDTH), lambda i: (i, 0)),
                pl.BlockSpec((tm, BRANCH_WIDTH), lambda i: (i, 0)),
                pl.BlockSpec((tm, d), lambda i: (i, gcol)),
                pl.BlockSpec((tm, d), lambda i: (i, gcol + 1)),
                pl.BlockSpec((tm, d), lambda i: (i, gcol + 2)),
                pl.BlockSpec((tm, d), lambda i: (i, 0)),
                pl.BlockSpec((1, N_ADA, d), lambda i: (i // per_b, 0, 0)),
                pl.BlockSpec((N_BRANCH, BRANCH_WIDTH, d), lambda i: (0, 0, 0)),
                pl.BlockSpec((d, d), lambda i: (0, 0)),
                pl.BlockSpec((1, d), lambda i: (0, 0))]
    args = [o_sb, o_mla, o_diff, proj, proj, proj, x2, mod, wb, wo, ng]
    out_shape = [jax.ShapeDtypeStruct((t, d), F32), jax.ShapeDtypeStruct((t, d), F32 if moe else BF16)]
    out_specs = [pl.BlockSpec((tm, d), lambda i: (i, 0)), pl.BlockSpec((tm, d), lambda i: (i, 0))]
    if moe:
        in_specs.append(pl.BlockSpec((d, 2 * LANES), lambda i: (0, 0)))
        args.append(wr)
        out_shape.append(jax.ShapeDtypeStruct((t, LANES), F32))
        out_specs.append(pl.BlockSpec((tm, LANES), lambda i: (i, 0)))
    return pl.pallas_call(
        functools.partial(_merge_kernel, moe=moe),
        out_shape=tuple(out_shape),
        grid=(t // tm,),
        in_specs=in_specs,
        out_specs=tuple(out_specs),
        compiler_params=_cparams("arbitrary"),
        name="merge_moe" if moe else "merge_dense",
    )(*args)


def _swiglu_partial(h, wg, wu, wd):
    g = _dot(h, wg)
    u = _dot(h, wu)
    return _dot((g * jax.nn.sigmoid(g) * u).astype(BF16), wd)


def _ffn_kernel(h_ref, wg_ref, wu_ref, wd_ref, x_ref, mod_ref, o_ref, acc_ref):
    j = pl.program_id(1)
    part = _swiglu_partial(h_ref[...], wg_ref[...], wu_ref[...], wd_ref[...])

    @pl.when(j == 0)
    def _():
        acc_ref[...] = part

    @pl.when(j > 0)
    def _():
        acc_ref[...] += part

    @pl.when(j == pl.num_programs(1) - 1)
    def _():
        o_ref[...] = x_ref[...] + mod_ref[0, 5:6, :] * acc_ref[...]


def _ff_chunk(d_ff):
    return 1408 if d_ff % 1408 == 0 else d_ff


def _ffn(h, wg, wu, wd, x2, mod, seq):
    t, d = x2.shape
    d_ff = wg.shape[1]
    tm = min(1024, seq)
    tf = _ff_chunk(d_ff)
    per_b = seq // tm
    return pl.pallas_call(
        _ffn_kernel,
        out_shape=jax.ShapeDtypeStruct((t, d), F32),
        grid=(t // tm, d_ff // tf),
        in_specs=[pl.BlockSpec((tm, d), lambda i, j: (i, 0)),
                  pl.BlockSpec((d, tf), lambda i, j: (0, j)),
                  pl.BlockSpec((d, tf), lambda i, j: (0, j)),
                  pl.BlockSpec((tf, d), lambda i, j: (j, 0)),
                  pl.BlockSpec((tm, d), lambda i, j: (i, 0)),
                  pl.BlockSpec((1, N_ADA, d), lambda i, j: (i // per_b, 0, 0))],
        out_specs=pl.BlockSpec((tm, d), lambda i, j: (i, 0)),
        scratch_shapes=[pltpu.VMEM((tm, d), F32)],
        compiler_params=_cparams("arbitrary", "arbitrary"),
        name="ffn_dense",
    )(h, wg, wu, wd, x2, mod)


def _gather_start(src_hbm, idx_ref, base, dst_ref, sem):
    for r in range(dst_ref.shape[0]):
        row = idx_ref[base + r]
        pltpu.make_async_copy(src_hbm.at[pl.ds(row, 1), :], dst_ref.at[pl.ds(r, 1), :], sem).start()


def _gather_wait(src_hbm, dst_ref, sem):
    pltpu.make_async_copy(src_hbm.at[pl.ds(0, dst_ref.shape[0]), :], dst_ref, sem).wait()


def _moe_kernel(te_ref, nu_ref, rt_ref, h_hbm, wg_ref, wu_ref, wd_ref, y_ref, xf_ref, xb_ref, acc_ref, sem):
    i = pl.program_id(0)
    j = pl.program_id(1)
    last = pl.num_programs(1) - 1
    n_tiles = pl.num_programs(0)
    tm = xb_ref.shape[0]
    valid = i < nu_ref[0]
    slot = i % 2

    @pl.when(j == 0)
    def _():
        @pl.when(i == 0)
        def _():
            _gather_start(h_hbm, rt_ref, 0, xf_ref.at[0], sem.at[0])

        _gather_wait(h_hbm, xf_ref.at[slot], sem.at[slot])
        xb_ref[...] = xf_ref[slot].astype(BF16)

        @pl.when(i + 1 < n_tiles)
        def _():
            _gather_start(h_hbm, rt_ref, (i + 1) * tm, xf_ref.at[1 - slot], sem.at[1 - slot])

    @pl.when(valid)
    def _():
        part = _swiglu_partial(xb_ref[...], wg_ref[0], wu_ref[0], wd_ref[0])

        @pl.when(j == 0)
        def _():
            acc_ref[...] = part

        @pl.when(j > 0)
        def _():
            acc_ref[...] += part

        @pl.when(j == last)
        def _():
            y_ref[...] = acc_ref[...]

    @pl.when(jnp.logical_and(jnp.logical_not(valid), j == last))
    def _():
        y_ref[...] = jnp.zeros_like(y_ref)


def _moe_experts(h, wg, wu, wd, tile_expert, n_used, row_token, tm):
    t, d = h.shape
    d_ff = wg.shape[2]
    tf = _ff_chunk(d_ff)
    nj = d_ff // tf
    n_tiles = tile_expert.shape[0]

    def chunk(i, j, nu):
        return jnp.where(i < nu[0], j, nj - 1)

    grid_spec = pltpu.PrefetchScalarGridSpec(
        num_scalar_prefetch=3,
        grid=(n_tiles, nj),
        in_specs=[pl.BlockSpec(memory_space=pl.ANY),
                  pl.BlockSpec((1, d, tf), lambda i, j, te, nu, rt: (te[i], 0, chunk(i, j, nu))),
                  pl.BlockSpec((1, d, tf), lambda i, j, te, nu, rt: (te[i], 0, chunk(i, j, nu))),
                  pl.BlockSpec((1, tf, d), lambda i, j, te, nu, rt: (te[i], chunk(i, j, nu), 0))],
        out_specs=pl.BlockSpec((tm, d), lambda i, j, te, nu, rt: (i, 0)),
        scratch_shapes=[pltpu.VMEM((2, tm, d), F32), pltpu.VMEM((tm, d), BF16), pltpu.VMEM((tm, d), F32),
                        pltpu.SemaphoreType.DMA((2,))],
    )
    return pl.pallas_call(
        _moe_kernel,
        out_shape=jax.ShapeDtypeStruct((n_tiles * tm, d), F32),
        grid_spec=grid_spec,
        compiler_params=_cparams("arbitrary", "arbitrary"),
        name="moe_experts",
    )(tile_expert, n_used, row_token, h, wg, wu, wd)


def _combine_kernel(slot_ref, y_hbm, route_ref, x_ref, mod_ref, o_ref, yb_ref, sem):
    i = pl.program_id(0)
    n = pl.num_programs(0)
    tm = x_ref.shape[0]
    rows = tm * TOP_K
    buf = i % 2

    @pl.when(i == 0)
    def _():
        _gather_start(y_hbm, slot_ref, 0, yb_ref.at[0], sem.at[0])

    @pl.when(i + 1 < n)
    def _():
        _gather_start(y_hbm, slot_ref, (i + 1) * rows, yb_ref.at[1 - buf], sem.at[1 - buf])

    _gather_wait(y_hbm, yb_ref.at[buf], sem.at[buf])
    route = route_ref[...]
    f = route[:, 2:3] * yb_ref[buf, 0:tm, :] + route[:, 3:4] * yb_ref[buf, tm:rows, :]
    o_ref[...] = x_ref[...] + mod_ref[0, 5:6, :] * f


def _combine(slot, y, route, x2, mod, seq):
    t, d = x2.shape
    tm = min(256, seq)
    per_b = seq // tm
    grid_spec = pltpu.PrefetchScalarGridSpec(
        num_scalar_prefetch=1,
        grid=(t // tm,),
        in_specs=[pl.BlockSpec(memory_space=pl.ANY),
                  pl.BlockSpec((tm, LANES), lambda i, sl: (i, 0)),
                  pl.BlockSpec((tm, d), lambda i, sl: (i, 0)),
                  pl.BlockSpec((1, N_ADA, d), lambda i, sl: (i // per_b, 0, 0))],
        out_specs=pl.BlockSpec((tm, d), lambda i, sl: (i, 0)),
        scratch_shapes=[pltpu.VMEM((2, TOP_K * tm, d), F32), pltpu.SemaphoreType.DMA((2,))],
    )
    slot_tiles = slot.reshape(t // tm, tm, TOP_K).transpose(0, 2, 1).reshape(-1)
    return pl.pallas_call(
        _combine_kernel,
        out_shape=jax.ShapeDtypeStruct((t, d), F32),
        grid_spec=grid_spec,
        compiler_params=_cparams("arbitrary"),
        name="moe_combine",
    )(slot_tiles, y, route, x2, mod)


def _routing_tables(route, tm):
    t = route.shape[0]
    p = t * TOP_K
    n_tiles = p // tm + N_EXPERTS
    e = route[:, :TOP_K].astype(jnp.int32).reshape(p)
    onehot = (e[:, None] == jnp.arange(N_EXPERTS, dtype=jnp.int32)[None, :]).astype(jnp.int32)
    csum = jnp.cumsum(onehot, axis=0)
    count = csum[-1]
    rank = jnp.take_along_axis(csum, e[:, None], axis=1)[:, 0] - 1
    tiles_e = (count + tm - 1) // tm
    tile_end = jnp.cumsum(tiles_e)
    tile_start = tile_end - tiles_e
    slot = tile_start[e] * tm + rank
    n_used = tile_end[-1]
    tile_id = jnp.minimum(jnp.arange(n_tiles, dtype=jnp.int32), n_used - 1)
    tile_expert = jnp.sum((tile_id[:, None] >= tile_end[None, :]).astype(jnp.int32), axis=1)
    row_token = jnp.zeros((n_tiles * tm,), jnp.int32).at[slot].set(
        jnp.arange(p, dtype=jnp.int32) // TOP_K, unique_indices=True)
    return tile_expert.astype(jnp.int32), n_used.reshape(1).astype(jnp.int32), row_token, slot.astype(jnp.int32)


def _pack_w_in(w):
    d = w.shape[0]
    sb_cols = 3 * SB_HEADS * HALF
    mla0 = sb_cols
    dif0 = mla0 + MLA_Q_RANK + MLA_KV_RANK + MLA_ROPE
    gate0 = dif0 + 3 * BRANCH_WIDTH
    z = lambda n: jnp.zeros((d, n), w.dtype)
    packed = jnp.concatenate([
        w[:, :sb_cols],
        w[:, mla0:mla0 + MLA_Q_RANK + MLA_KV_RANK],
        w[:, dif0:gate0],
        w[:, gate0:],
        z(HALF), w[:, mla0 + MLA_Q_RANK + MLA_KV_RANK:dif0], z(LANES - HALF - MLA_ROPE)], axis=1)
    assert packed.shape[1] == IN_COLS_PACKED
    return packed.astype(BF16)


def _pack_w_uq(w):
    r = w.shape[0]
    half = MLA_ROPE // 2
    w3 = w.reshape(r, MLA_HEADS, MLA_QK)
    z = lambda n: jnp.zeros((r, MLA_HEADS, n), w.dtype)
    plain = jnp.concatenate([w3, z(LANES - MLA_QK)], axis=2)
    swapped = jnp.concatenate([z(MLA_NOPE), w3[:, :, MLA_NOPE + half:], w3[:, :, MLA_NOPE:MLA_NOPE + half],
                               z(LANES - MLA_QK)], axis=2)
    return jnp.concatenate([plain.reshape(r, MLA_HEADS * LANES), swapped.reshape(r, MLA_HEADS * LANES)],
                           axis=1).astype(BF16)


def _pack_w_ukv(w):
    r = w.shape[0]
    w3 = w.reshape(r, MLA_HEADS, MLA_NOPE + HALF)
    wk = jnp.concatenate([w3[:, :, :MLA_NOPE], jnp.zeros((r, MLA_HEADS, LANES - MLA_NOPE), w.dtype)], axis=2)
    wv = w3[:, :, MLA_NOPE:]
    return wk.reshape(r, MLA_HEADS * LANES).astype(BF16), wv.reshape(r, MLA_HEADS * HALF).astype(BF16)


def _head_lane_gain(g):
    return jnp.concatenate([g, jnp.zeros((LANES - MLA_QK,), g.dtype)])


def _prep_table(mla_q_gain, mla_k_gain, diff_q_gain, diff_k_gain):
    half = MLA_ROPE // 2
    inv_freq = ROPE_THETA ** (-jnp.arange(half, dtype=F32) / half)
    zeros = lambda n: jnp.zeros((n,), F32)
    freq = jnp.concatenate([zeros(MLA_NOPE), inv_freq, inv_freq, zeros(LANES - MLA_QK)])
    sgn_lo = jnp.concatenate([zeros(MLA_NOPE), -jnp.ones((half,), F32), zeros(LANES - MLA_NOPE - half)])
    sgn_hi = jnp.concatenate([zeros(MLA_NOPE + half), jnp.ones((half,), F32), zeros(LANES - MLA_QK)])
    log2e = math.log2(math.e)
    rows = [freq, sgn_lo, sgn_hi,
            _head_lane_gain(mla_q_gain) * (log2e / math.sqrt(MLA_QK)),
            _head_lane_gain(mla_k_gain),
            jnp.tile(diff_q_gain, 2) * (log2e / math.sqrt(HALF)),
            jnp.tile(diff_k_gain, 2),
            zeros(LANES)]
    return jnp.stack(rows).astype(F32)


def kernel(x, c, positions, norm1_g, norm2_g, w_ada, b_ada, w_in, mla_q_norm, w_mla_uq, mla_kv_norm,
           w_mla_ukv, mla_q_gain, mla_k_gain, diff_q_gain, diff_k_gain, diff_lambda, diff_subln,
           w_branch, w_out, w_ffn_gate, w_ffn_up, w_ffn_down, w_router, w_exp_gate, w_exp_up, w_exp_down):
    batch, seq, d = x.shape
    depth = w_in.shape[0]
    t = batch * seq
    blk = min(256, seq)
    x2 = x.reshape(t, d)
    pos_col = positions.reshape(t, 1).astype(jnp.int32)
    pos_blk = positions.reshape(batch, seq // blk, 1, blk).astype(jnp.int32)
    slopes = 2.0 ** (-8.0 * jnp.arange(1, DIFF_HEADS + 1, dtype=F32) / DIFF_HEADS)
    mods = _ada(c, w_ada, b_ada).reshape(depth, batch, N_ADA, d)
    moe_tm = min(512, seq)

    for layer in range(depth):
        mod = mods[layer]
        proj = _in_proj(x2, norm1_g[layer].reshape(1, d), mod, _pack_w_in(w_in[layer]), seq)
        wuk, wuv = _pack_w_ukv(w_mla_ukv[layer])
        tab = _prep_table(mla_q_gain[layer], mla_k_gain[layer], diff_q_gain[layer], diff_k_gain[layer])
        qm, km, vm, dqn, dkn = _prep(proj, pos_col, tab, mla_q_norm[layer].reshape(1, -1),
                                     mla_kv_norm[layer].reshape(1, -1), _pack_w_uq(w_mla_uq[layer]),
                                     wuk, wuv, batch, seq)
        o_sb = _sb_attention(proj, batch, seq)
        o_mla = _mla_attention(qm, km, vm, batch, seq)
        lam_init = 0.8 - 0.6 * math.exp(-0.3 * layer)
        o_diff = _diff_attention(dqn, dkn, proj, pos_col, pos_blk, diff_lambda[layer],
                                 diff_subln[layer].reshape(1, LANES), slopes, lam_init, batch, seq)
        j = layer // 2
        moe = layer % 2 == 1
        wr = None
        if moe:
            wr = jnp.concatenate([w_router[j], jnp.zeros((d, LANES - N_EXPERTS), F32)], axis=1)
            wr_hi = wr.astype(BF16)
            wr = jnp.concatenate([wr_hi, (wr - wr_hi.astype(F32)).astype(BF16)], axis=1)
        outs = _merge(o_sb, o_mla, o_diff, proj, x2, mod, w_branch[layer].astype(BF16),
                      w_out[layer].astype(BF16), norm2_g[layer].reshape(1, d), wr, seq, moe)
        if moe:
            x_mid, h2, route = outs
            tile_expert, n_used, row_token, slot = _routing_tables(route, moe_tm)
            y = _moe_experts(h2, w_exp_gate[j].astype(BF16), w_exp_up[j].astype(BF16),
                             w_exp_down[j].astype(BF16), tile_expert, n_used, row_token, moe_tm)
            x2 = _combine(slot, y, route, x_mid, mod, seq)
        else:
            x_mid, h2 = outs
            x2 = _ffn(h2, w_ffn_gate[j].astype(BF16), w_ffn_up[j].astype(BF16),
                      w_ffn_down[j].astype(BF16), x_mid, mod, seq)
    return x2.reshape(batch, seq, d)
```

```python
import functools
import math

import jax
import jax.numpy as jnp
from jax import lax
from jax.experimental import pallas as pl
from jax.experimental.pallas import tpu as pltpu

F32 = jnp.float32
BF16 = jnp.bfloat16
HIGHEST = lax.Precision.HIGHEST

NORM_EPS = 1e-6
LANES = 128
HALF = 64

SB_HEADS = 8
MLA_HEADS = 8
MLA_Q_RANK = 768
MLA_KV_RANK = 256
MLA_NOPE = 64
MLA_ROPE = 32
MLA_QK = MLA_NOPE + MLA_ROPE
ROPE_THETA = 10000.0
DIFF_HEADS = 4
BRANCH_WIDTH = 512
N_BRANCH = 3
N_EXPERTS = 8
TOP_K = 2
N_ADA = 6
NEG_BIG = -1e30
ONES_ROWS = 16

COL_SB = 0
COL_CQ = 1536
COL_CKV = 2304
COL_DIFF = 2560
COL_GATE = 4096
COL_KROPE = 7168
IN_COLS_PACKED = 7296

VMEM_LIMIT = 56 * 1024 * 1024


def _cparams(*sem):
    return pltpu.CompilerParams(dimension_semantics=sem, vmem_limit_bytes=VMEM_LIMIT)


def _dot(a, b):
    return jnp.dot(a, b, preferred_element_type=F32)


def _dot_nt(a, b):
    return lax.dot_general(a, b, (((1,), (1,)), ((), ())), preferred_element_type=F32)


def _norm_mod(x, g, sc, sh):
    ms = jnp.mean(x * x, axis=-1, keepdims=True)
    return (x * lax.rsqrt(ms + NORM_EPS) * g) * (1.0 + sc) + sh


def _ada_kernel(c_ref, w_ref, b_ref, o_ref):
    c = c_ref[...]
    cond = c * jax.nn.sigmoid(c)
    o_ref[0] = jnp.dot(cond, w_ref[0], precision=HIGHEST, preferred_element_type=F32) + b_ref[0]


def _ada(c, w_ada, b_ada):
    depth, d, n = w_ada.shape
    b = c.shape[0]
    tn = 1536
    return pl.pallas_call(
        _ada_kernel,
        out_shape=jax.ShapeDtypeStruct((depth, b, n), F32),
        grid=(depth, n // tn),
        in_specs=[pl.BlockSpec((b, d), lambda l, j: (0, 0)),
                  pl.BlockSpec((1, d, tn), lambda l, j: (l, 0, j)),
                  pl.BlockSpec((1, 1, tn), lambda l, j: (l, 0, j))],
        out_specs=pl.BlockSpec((1, b, tn), lambda l, j: (l, 0, j)),
        compiler_params=_cparams("arbitrary", "arbitrary"),
        name="ada_mod",
    )(c, w_ada, b_ada.reshape(depth, 1, n))


def _in_proj_kernel(x_ref, g_ref, mod_ref, w_ref, o_ref):
    h = _norm_mod(x_ref[...], g_ref[...], mod_ref[0, 1:2, :], mod_ref[0, 0:1, :])
    o_ref[...] = _dot(h.astype(BF16), w_ref[...]).astype(o_ref.dtype)


def _in_proj(x2, g, mod, w, seq):
    t, d = x2.shape
    n = w.shape[1]
    tm = min(1024, seq)
    tn = 2432
    per_b = seq // tm
    return pl.pallas_call(
        _in_proj_kernel,
        out_shape=jax.ShapeDtypeStruct((t, n), BF16),
        grid=(t // tm, n // tn),
        in_specs=[pl.BlockSpec((tm, d), lambda i, j: (i, 0)),
                  pl.BlockSpec((1, d), lambda i, j: (0, 0)),
                  pl.BlockSpec((1, N_ADA, d), lambda i, j: (i // per_b, 0, 0)),
                  pl.BlockSpec((d, tn), lambda i, j: (0, j))],
        out_specs=pl.BlockSpec((tm, tn), lambda i, j: (i, j)),
        compiler_params=_cparams("arbitrary", "arbitrary"),
        name="in_proj",
    )(x2, g, mod, w)


def _prep_kernel(cq_ref, ckv_ref, kr_ref, dq_ref, dk_ref, pos_ref, tab_ref, qn_ref, kvn_ref,
                 wuq_ref, wuk_ref, wuv_ref, qm_ref, km_ref, vm_ref, dqn_ref, dkn_ref):
    tm = cq_ref.shape[0]
    lane = lax.broadcasted_iota(jnp.int32, (tm, LANES), 1)
    freq = tab_ref[0:1, :]
    sgn_lo = tab_ref[1:2, :]
    sgn_hi = tab_ref[2:3, :]
    gq = tab_ref[3:4, :]
    gk = tab_ref[4:5, :]
    gdq = tab_ref[5:6, :]
    gdk = tab_ref[6:7, :]

    ang = pos_ref[...].astype(F32) * freq
    cos = jnp.cos(ang)
    sin = jnp.sin(ang)
    s_lo = sin * sgn_lo
    s_hi = sin * sgn_hi
    s_pm = s_lo + s_hi

    def qk_norm(r, gain):
        ms = jnp.sum(r * r, axis=-1, keepdims=True) * (1.0 / MLA_QK)
        return (r * lax.rsqrt(ms + NORM_EPS) * gain).astype(BF16)

    cq = cq_ref[...].astype(F32)
    cqn = (cq * lax.rsqrt(jnp.mean(cq * cq, axis=-1, keepdims=True) + NORM_EPS) * qn_ref[...]).astype(BF16)
    q = _dot(cqn, wuq_ref[...])
    ckv = ckv_ref[...].astype(F32)
    ckvn = (ckv * lax.rsqrt(jnp.mean(ckv * ckv, axis=-1, keepdims=True) + NORM_EPS) * kvn_ref[...]).astype(BF16)
    kn = _dot(ckvn, wuk_ref[...])
    vm_ref[...] = _dot(ckvn, wuv_ref[...]).astype(BF16)
    kr = kr_ref[...].astype(F32)
    kr = kr * cos + pltpu.roll(kr, LANES - MLA_ROPE // 2, 1) * s_lo + pltpu.roll(kr, MLA_ROPE // 2, 1) * s_hi
    nh = MLA_HEADS
    for h in range(nh):
        qh = q[:, h * LANES:(h + 1) * LANES] * cos + q[:, (nh + h) * LANES:(nh + h + 1) * LANES] * s_pm
        qm_ref[0, h] = qk_norm(qh, gq)
        km_ref[0, h] = qk_norm(kn[:, h * LANES:(h + 1) * LANES] + kr, gk)

    def half_norm(ref, gain, out_ref):
        for cb in range(ref.shape[1] // LANES):
            v = ref[:, cb * LANES:(cb + 1) * LANES].astype(F32)
            sq = v * v
            lo = jnp.sum(jnp.where(lane < HALF, sq, 0.0), axis=-1, keepdims=True)
            hi = jnp.sum(jnp.where(lane >= HALF, sq, 0.0), axis=-1, keepdims=True)
            ms = jnp.where(lane < HALF, lo, hi) * (1.0 / HALF)
            out_ref[:, cb * LANES:(cb + 1) * LANES] = (v * lax.rsqrt(ms + NORM_EPS) * gain).astype(BF16)

    half_norm(dq_ref, gdq, dqn_ref)
    half_norm(dk_ref, gdk, dkn_ref)


def _prep(proj, pos_col, tab, qn, kvn, wuq, wuk, wuv, batch, seq):
    t = proj.shape[0]
    tm = min(512, seq)
    per_b = seq // tm
    h = MLA_HEADS
    const = lambda shape: pl.BlockSpec(shape, lambda i: (0,) * len(shape))
    return pl.pallas_call(
        _prep_kernel,
        out_shape=(jax.ShapeDtypeStruct((batch, h, seq, LANES), BF16),
                   jax.ShapeDtypeStruct((batch, h, seq, LANES), BF16),
                   jax.ShapeDtypeStruct((t, BRANCH_WIDTH), BF16),
                   jax.ShapeDtypeStruct((t, BRANCH_WIDTH), BF16),
                   jax.ShapeDtypeStruct((t, BRANCH_WIDTH), BF16)),
        grid=(t // tm,),
        in_specs=[pl.BlockSpec((tm, MLA_Q_RANK), lambda i: (i, COL_CQ // MLA_Q_RANK)),
                  pl.BlockSpec((tm, MLA_KV_RANK), lambda i: (i, COL_CKV // MLA_KV_RANK)),
                  pl.BlockSpec((tm, LANES), lambda i: (i, COL_KROPE // LANES)),
                  pl.BlockSpec((tm, BRANCH_WIDTH), lambda i: (i, COL_DIFF // BRANCH_WIDTH)),
                  pl.BlockSpec((tm, BRANCH_WIDTH), lambda i: (i, COL_DIFF // BRANCH_WIDTH + 1)),
                  pl.BlockSpec((tm, 1), lambda i: (i, 0)),
                  const((8, LANES)), const((1, MLA_Q_RANK)), const((1, MLA_KV_RANK)),
                  const((MLA_Q_RANK, 2 * h * LANES)), const((MLA_KV_RANK, h * LANES)),
                  const((MLA_KV_RANK, BRANCH_WIDTH))],
        out_specs=(pl.BlockSpec((1, h, tm, LANES), lambda i: (i // per_b, 0, i % per_b, 0)),
                   pl.BlockSpec((1, h, tm, LANES), lambda i: (i // per_b, 0, i % per_b, 0)),
                   pl.BlockSpec((tm, BRANCH_WIDTH), lambda i: (i, 0)),
                   pl.BlockSpec((tm, BRANCH_WIDTH), lambda i: (i, 0)),
                   pl.BlockSpec((tm, BRANCH_WIDTH), lambda i: (i, 0))),
        compiler_params=_cparams("arbitrary"),
        name="attn_prep",
    )(proj, proj, proj, proj, proj, pos_col, tab, qn, kvn, wuq, wuk, wuv)


def _causal_items(nq, backward):
    qb, kb, dg, first = [], [], [], []
    for qi in range(nq):
        order = range(qi, -1, -1) if backward else range(qi + 1)
        for n, j in enumerate(order):
            qb.append(qi), kb.append(j), dg.append(int(j == qi)), first.append(int(n == 0))
    as_arr = lambda v: jnp.asarray(v, dtype=jnp.int32)
    return as_arr(qb), as_arr(kb), as_arr(dg), as_arr(first)


PIPE_SLOTS = 4


def _run_pipeline(n, phases, carry):
    depth = len(phases)

    def iteration(u, carry, u_mod):
        for d in range(depth):
            carry = phases[d](u - d, (u_mod - d) % PIPE_SLOTS, carry)
        return carry

    for u in range(depth - 1):
        for d in range(u + 1):
            carry = phases[d](u - d, (u - d) % PIPE_SLOTS, carry)
    start = depth - 1
    for _ in range((n - start) % PIPE_SLOTS):
        carry = iteration(start, carry, start % PIPE_SLOTS)
        start += 1

    def body(t, carry):
        u = start + PIPE_SLOTS * t
        for r in range(PIPE_SLOTS):
            carry = iteration(u + r, carry, (start + r) % PIPE_SLOTS)
        return carry

    carry = lax.fori_loop(0, (n - start) // PIPE_SLOTS, body, carry)
    for u in range(n, n + depth - 1):
        for d in range(u - n + 1, depth):
            carry = phases[d](u - d, (u - d) % PIPE_SLOTS, carry)
    return carry


def _first_step():
    return jnp.logical_and(pl.program_id(0) == 0, pl.program_id(1) == 0)


def _sb_kernel(tq_ref, tk_ref, td_ref, tf_ref, q_ref, k_ref, v_ref, o_ref, q_s, acc_s, mbias_s, tri_s,
               *bufs, blk, nq, n_items):
    z_s, hl_s, w_s = bufs[0:4], bufs[4:8], bufs[8:12]
    lane = lax.broadcasted_iota(jnp.int32, (blk, LANES), 1)

    @pl.when(_first_step())
    def _():
        row = lax.broadcasted_iota(jnp.int32, (blk, blk), 0)
        col = lax.broadcasted_iota(jnp.int32, (blk, blk), 1)
        mbias_s[0] = jnp.zeros((blk, blk), F32)
        mbias_s[1] = jnp.where(col < row, 0.0, NEG_BIG)
        tri = jnp.where(row >= col, 1.0, 0.0).astype(BF16)
        tri_s[0:blk, :] = tri
        tri_s[blk:2 * blk, :] = tri

    scale = math.log2(math.e) / math.sqrt(HALF)
    for qi in range(nq):
        rows = slice(qi * blk, (qi + 1) * blk)
        q = (q_ref[rows, :].astype(F32) * scale).astype(BF16)
        q_s[0, rows, :] = jnp.where(lane < HALF, q, jnp.zeros_like(q))
        q_s[1, rows, :] = jnp.where(lane >= HALF, q, jnp.zeros_like(q))
    acc_s[...] = jnp.zeros(acc_s.shape, F32)

    def rows_of(block):
        return pl.ds(pl.multiple_of(block * blk, blk), blk)

    def scores(k, slot, c):
        par = slot % 2
        kb = k_ref[rows_of(tk_ref[k]), :]
        mbias = mbias_s[td_ref[k]]
        qrows = rows_of(tq_ref[k])
        for s in range(2):
            z = _dot_nt(q_s[s, qrows, :], kb) + mbias
            nz = -z
            sp = jnp.log2(1.0 + jnp.exp2(jnp.minimum(z, nz)))
            lneg = jnp.minimum(nz, 0.0) - sp
            hi = lneg.astype(BF16)
            hl_s[2 * par + s][:, 0:blk] = hi
            hl_s[2 * par + s][:, blk:2 * blk] = (lneg - hi.astype(F32)).astype(BF16)
            z_s[2 * par + s][...] = z
        return c

    def weights(k, slot, c):
        par = slot % 2
        diag = td_ref[k]
        out = []
        for s in range(2):
            incl = _dot(hl_s[2 * par + s][...], tri_s[...])
            cs = jnp.where(diag == 1, 0.0, c[s])
            w_s[2 * par + s][...] = jnp.exp2(z_s[2 * par + s][...] + incl + cs).astype(BF16)
            out.append(cs + incl[:, 0:1])
        return tuple(out)

    def values(k, slot, c):
        par = slot % 2
        vb = v_ref[rows_of(tk_ref[k]), :]
        qb = tq_ref[k]
        for s in range(2):
            acc_s[qb, s] += _dot(w_s[2 * par + s][...], vb)
        return c

    c = (jnp.zeros((blk, 1), F32), jnp.zeros((blk, 1), F32))
    _run_pipeline(n_items, (scores, weights, values), c)
    for qi in range(nq):
        o_ref[qi * blk:(qi + 1) * blk, :] = jnp.where(lane < HALF, acc_s[qi, 0], acc_s[qi, 1]).astype(o_ref.dtype)


def _sb_attention(proj, batch, seq):
    t = proj.shape[0]
    blk = min(256, seq)
    nq = seq // blk
    npair = SB_HEADS // 2
    qcol = COL_SB // LANES
    kcol = qcol + npair
    vcol = qcol + 2 * npair
    tables = _causal_items(nq, backward=True)
    grid_spec = pltpu.PrefetchScalarGridSpec(
        num_scalar_prefetch=4,
        grid=(batch, npair),
        in_specs=[pl.BlockSpec((seq, LANES), lambda b, p, *_: (b, qcol + p)),
                  pl.BlockSpec((seq, LANES), lambda b, p, *_: (b, kcol + p)),
                  pl.BlockSpec((seq, LANES), lambda b, p, *_: (b, vcol + p))],
        out_specs=pl.BlockSpec((seq, LANES), lambda b, p, *_: (b, p)),
        scratch_shapes=[pltpu.VMEM((2, seq, LANES), BF16),
                        pltpu.VMEM((nq, 2, blk, LANES), F32),
                        pltpu.VMEM((2, blk, blk), F32),
                        pltpu.VMEM((2 * blk, blk), BF16)]
        + [pltpu.VMEM((blk, blk), F32)] * 4
        + [pltpu.VMEM((blk, 2 * blk), BF16)] * 4
        + [pltpu.VMEM((blk, blk), BF16)] * 4,
    )
    return pl.pallas_call(
        functools.partial(_sb_kernel, blk=blk, nq=nq, n_items=int(tables[0].shape[0])),
        out_shape=jax.ShapeDtypeStruct((t, BRANCH_WIDTH), BF16),
        grid_spec=grid_spec,
        compiler_params=_cparams("arbitrary", "arbitrary"),
        name="sb_attention",
    )(*tables, proj, proj, proj)


def _init_causal_bias_t(mbias_s, blk):
    key = lax.broadcasted_iota(jnp.int32, (blk, blk), 0)
    qry = lax.broadcasted_iota(jnp.int32, (blk, blk), 1)
    mbias_s[0] = jnp.zeros((blk, blk), F32)
    mbias_s[1] = jnp.where(key <= qry, 0.0, NEG_BIG)


def _softmax_pipeline(tables, score_fn, vt_fn, acc_s, mbias_s, bufs, *, blk, n_items):
    tq_ref, tk_ref, td_ref, tf_ref = tables
    s_s, p_s, m_s, a_s, a2_s = bufs[0:4], bufs[4:8], bufs[8:12], bufs[12:16], bufs[16:20]

    def rows_of(block):
        return pl.ds(pl.multiple_of(block * blk, blk), blk)

    def scores(k, slot, carry):
        par = slot % 2
        first = tf_ref[k] == 1
        mbias = mbias_s[td_ref[k]]
        sc_pair = score_fn(k, rows_of(tq_ref[k]), rows_of(tk_ref[k]))
        m_out = []
        for s in range(2):
            m_prev = jnp.where(first, NEG_BIG, carry[s])
            sc = sc_pair[s] + mbias
            m_new = jnp.maximum(m_prev, jnp.max(sc, axis=0, keepdims=True))
            s_s[2 * par + s][...] = sc
            m_s[2 * par + s][...] = m_new
            a_s[2 * par + s][...] = jnp.exp2(m_prev - m_new)
            m_out.append(m_new)
        return tuple(m_out)

    def probabilities(k, slot, carry):
        par = slot % 2
        for s in range(2):
            p_s[2 * par + s][...] = jnp.exp2(s_s[2 * par + s][...] - m_s[2 * par + s][...]).astype(BF16)
            a2_s[2 * par + s][...] = a_s[2 * par + s][...]
        return carry

    def values(k, slot, carry):
        par = slot % 2
        kb = tk_ref[k]
        qb = tq_ref[k]
        for s in range(2):
            acc_s[qb, s] = a2_s[2 * par + s][...] * acc_s[qb, s] + _dot(vt_fn(s, kb), p_s[2 * par + s][...])
        return carry

    m0 = jnp.full((1, blk), NEG_BIG, F32)
    _run_pipeline(n_items, (scores, probabilities, values), (m0, m0))


def _softmax_scratch(blk, nq, acc_rows):
    return ([pltpu.VMEM((nq, 2, acc_rows, blk), F32), pltpu.VMEM((2, blk, blk), F32)]
            + [pltpu.VMEM((blk, blk), F32)] * 4 + [pltpu.VMEM((blk, blk), BF16)] * 4
            + [pltpu.VMEM((1, blk), F32)] * 12)


def _mla_kernel(tq_ref, tk_ref, td_ref, tf_ref, q_ref, k_ref, v_ref, o_ref, vt_s, acc_s, mbias_s, *bufs,
                blk, nq):
    vdim = lax.broadcasted_iota(jnp.int32, (LANES, blk), 0)

    @pl.when(_first_step())
    def _():
        _init_causal_bias_t(mbias_s, blk)

    acc_s[...] = jnp.zeros(acc_s.shape, F32)
    for kb in range(nq):
        vt = v_ref[kb * blk:(kb + 1) * blk, :].astype(F32).T
        vt_s[0, kb] = jnp.where(vdim < HALF, vt, 1.0).astype(BF16)
        vt_s[1, kb] = jnp.where(vdim >= HALF, vt, 1.0).astype(BF16)

    def score_fn(k, qrows, krows):
        return tuple(_dot_nt(k_ref[0, s, krows, :], q_ref[0, s, qrows, :]) for s in range(2))

    _softmax_pipeline((tq_ref, tk_ref, td_ref, tf_ref), score_fn, lambda s, kb: vt_s[s, kb], acc_s, mbias_s,
                      bufs, blk=blk, n_items=nq * (nq + 1) // 2)
    for qi in range(nq):
        a0 = acc_s[qi, 0]
        a1 = acc_s[qi, 1]
        o = jnp.where(vdim < HALF, a0 / a0[HALF:HALF + 1, :], a1 / a1[0:1, :])
        o_ref[qi * blk:(qi + 1) * blk, :] = o.T.astype(o_ref.dtype)


def _mla_attention(qm, km, vm, batch, seq):
    t = vm.shape[0]
    blk = min(256, seq)
    nq = seq // blk
    npair = MLA_HEADS // 2
    grid_spec = pltpu.PrefetchScalarGridSpec(
        num_scalar_prefetch=4,
        grid=(batch, npair),
        in_specs=[pl.BlockSpec((1, 2, seq, LANES), lambda b, p, *_: (b, p, 0, 0)),
                  pl.BlockSpec((1, 2, seq, LANES), lambda b, p, *_: (b, p, 0, 0)),
                  pl.BlockSpec((seq, LANES), lambda b, p, *_: (b, p))],
        out_specs=pl.BlockSpec((seq, LANES), lambda b, p, *_: (b, p)),
        scratch_shapes=[pltpu.VMEM((2, nq, LANES, blk), BF16)] + _softmax_scratch(blk, nq, LANES),
    )
    return pl.pallas_call(
        functools.partial(_mla_kernel, blk=blk, nq=nq),
        out_shape=jax.ShapeDtypeStruct((t, BRANCH_WIDTH), BF16),
        grid_spec=grid_spec,
        compiler_params=_cparams("arbitrary", "arbitrary"),
        name="mla_attention",
    )(*_causal_items(nq, backward=False), qm, km, vm)


def _diff_kernel(tq_ref, tk_ref, td_ref, tf_ref, slope_ref, q_ref, k_ref, v_ref, pcol_ref, prow_ref, lam_ref, g_ref,
                 o_ref, q_s, vt_s, pc_s, pr_s, acc_s, mbias_s, *bufs, blk, nq, lam_init):
    lane = lax.broadcasted_iota(jnp.int32, (blk, LANES), 1)

    @pl.when(_first_step())
    def _():
        _init_causal_bias_t(mbias_s, blk)

    acc_s[...] = jnp.zeros(acc_s.shape, F32)
    slope = slope_ref[pl.program_id(1)] * math.log2(math.e)
    pc_s[...] = pcol_ref[...].astype(F32) * slope
    pr_s[...] = prow_ref[0].astype(F32) * slope
    for qi in range(nq):
        rows = slice(qi * blk, (qi + 1) * blk)
        q = q_ref[rows, :]
        q_s[0, rows, :] = jnp.where(lane < HALF, q, jnp.zeros_like(q))
        q_s[1, rows, :] = jnp.where(lane >= HALF, q, jnp.zeros_like(q))
        vt_s[qi, 0:LANES, :] = v_ref[rows, :].astype(F32).T.astype(BF16)
        vt_s[qi, LANES:LANES + ONES_ROWS, :] = jnp.ones((ONES_ROWS, blk), BF16)

    def score_fn(k, qrows, krows):
        kb = k_ref[krows, :]
        bias = jnp.abs(pc_s[krows, :] - pr_s[tq_ref[k]])
        return tuple(_dot_nt(kb, q_s[s, qrows, :]) - bias for s in range(2))

    _softmax_pipeline((tq_ref, tk_ref, td_ref, tf_ref), score_fn, lambda s, kb: vt_s[kb], acc_s, mbias_s,
                      bufs, blk=blk, n_items=nq * (nq + 1) // 2)

    lp = lam_ref[...]
    e1 = jnp.exp(jnp.sum(lp[0:1, :] * lp[1:2, :], axis=-1, keepdims=True))
    e2 = jnp.exp(jnp.sum(lp[2:3, :] * lp[3:4, :], axis=-1, keepdims=True))
    lam = e1 - e2 + lam_init
    for qi in range(nq):
        a0 = acc_s[qi, 0]
        a1 = acc_s[qi, 1]
        o = (a0[0:LANES, :] / a0[LANES:LANES + 1, :] - lam * (a1[0:LANES, :] / a1[LANES:LANES + 1, :])).T
        ms = jnp.mean(o * o, axis=-1, keepdims=True)
        o_ref[qi * blk:(qi + 1) * blk, :] = (
            o * lax.rsqrt(ms + NORM_EPS) * g_ref[...] * (1.0 - lam_init)).astype(o_ref.dtype)


def _diff_attention(dqn, dkn, proj, pos_col, pos_blk, diff_lambda, subln, slopes, lam_init, batch, seq):
    t = dqn.shape[0]
    blk = min(256, seq)
    nq = seq // blk
    vcol = (COL_DIFF + 2 * BRANCH_WIDTH) // LANES
    grid_spec = pltpu.PrefetchScalarGridSpec(
        num_scalar_prefetch=5,
        grid=(batch, DIFF_HEADS),
        in_specs=[pl.BlockSpec((seq, LANES), lambda b, h, *_: (b, h)),
                  pl.BlockSpec((seq, LANES), lambda b, h, *_: (b, h)),
                  pl.BlockSpec((seq, LANES), lambda b, h, *_: (b, vcol + h)),
                  pl.BlockSpec((seq, 1), lambda b, h, *_: (b, 0)),
                  pl.BlockSpec((1, nq, 1, blk), lambda b, h, *_: (b, 0, 0, 0)),
                  pl.BlockSpec((4, HALF), lambda b, h, *_: (0, 0)),
                  pl.BlockSpec((1, LANES), lambda b, h, *_: (0, 0))],
        out_specs=pl.BlockSpec((seq, LANES), lambda b, h, *_: (b, h)),
        scratch_shapes=[pltpu.VMEM((2, seq, LANES), BF16), pltpu.VMEM((nq, LANES + ONES_ROWS, blk), BF16),
                        pltpu.VMEM((seq, 1), F32), pltpu.VMEM((nq, 1, blk), F32)]
        + _softmax_scratch(blk, nq, LANES + ONES_ROWS),
    )
    return pl.pallas_call(
        functools.partial(_diff_kernel, blk=blk, nq=nq, lam_init=lam_init),
        out_shape=jax.ShapeDtypeStruct((t, BRANCH_WIDTH), BF16),
        grid_spec=grid_spec,
        compiler_params=_cparams("arbitrary", "arbitrary"),
        name="diff_attention",
    )(*_causal_items(nq, backward=False), slopes, dqn, dkn, proj, pos_col, pos_blk, diff_lambda, subln)


def _merge_kernel(*refs, moe):
    (osb_ref, omla_ref, odiff_ref, g0_ref, g1_ref, g2_ref, x_ref, mod_ref, wb_ref, wo_ref, ng_ref) = refs[:11]
    if moe:
        wr_ref, xo_ref, h_ref, route_ref = refs[11:]
    else:
        xo_ref, h_ref = refs[11:]
    merged = None
    for n, (o_ref, g_ref) in enumerate(((osb_ref, g0_ref), (omla_ref, g1_ref), (odiff_ref, g2_ref))):
        y = jax.nn.sigmoid(g_ref[...].astype(F32)) * _dot(o_ref[...], wb_ref[n])
        merged = y if merged is None else merged + y
    mix = _dot(merged.astype(BF16), wo_ref[...])
    xn = x_ref[...] + mod_ref[0, 2:3, :] * mix
    xo_ref[...] = xn
    h = _norm_mod(xn, ng_ref[...], mod_ref[0, 4:5, :], mod_ref[0, 3:4, :])
    h_ref[...] = h.astype(h_ref.dtype)
    if moe:
        tm = h.shape[0]
        lane = lax.broadcasted_iota(jnp.int32, (tm, LANES), 1)
        lane_f = lane.astype(F32)
        h_hi = h.astype(BF16)
        h_lo = (h - h_hi.astype(F32)).astype(BF16)
        both = _dot(h_hi, wr_ref[...])
        logits = both[:, 0:LANES] + both[:, LANES:2 * LANES] + _dot(h_lo, wr_ref[:, 0:LANES])
        lg = jnp.where(lane < N_EXPERTS, logits, NEG_BIG)
        m1 = jnp.max(lg, axis=-1, keepdims=True)
        i1 = jnp.min(jnp.where(lg == m1, lane_f, float(LANES)), axis=-1, keepdims=True)
        lg2 = jnp.where(lane_f == i1, NEG_BIG, lg)
        m2 = jnp.max(lg2, axis=-1, keepdims=True)
        i2 = jnp.min(jnp.where(lg2 == m2, lane_f, float(LANES)), axis=-1, keepdims=True)
        e = jnp.exp(m2 - m1)
        w1 = 1.0 / (1.0 + e)
        w2 = e / (1.0 + e)
        route_ref[...] = jnp.where(lane == 0, i1, jnp.where(lane == 1, i2, jnp.where(
            lane == 2, w1, jnp.where(lane == 3, w2, 0.0))))


def _merge(o_sb, o_mla, o_diff, proj, x2, mod, wb, wo, ng, wr, seq, moe):
    t, d = x2.shape
    tm = min(512, seq)
    per_b = seq // tm
    gcol = COL_GATE // d
    in_specs = [pl.BlockSpec((tm, BRANCH_WIDTH), lambda i: (i, 0)),
                pl.BlockSpec((tm, BRANCH_WIDTH), lambda i: (i, 0)),
                pl.BlockSpec((tm, BRANCH_WIDTH), lambda i: (i, 0)),
                pl.BlockSpec((tm, d), lambda i: (i, gcol)),
                pl.BlockSpec((tm, d), lambda i: (i, gcol + 1)),
                pl.BlockSpec((tm, d), lambda i: (i, gcol + 2)),
                pl.BlockSpec((tm, d), lambda i: (i, 0)),
                pl.BlockSpec((1, N_ADA, d), lambda i: (i // per_b, 0, 0)),
                pl.BlockSpec((N_BRANCH, BRANCH_WIDTH, d), lambda i: (0, 0, 0)),
                pl.BlockSpec((d, d), lambda i: (0, 0)),
                pl.BlockSpec((1, d), lambda i: (0, 0))]
    args = [o_sb, o_mla, o_diff, proj, proj, proj, x2, mod, wb, wo, ng]
    out_shape = [jax.ShapeDtypeStruct((t, d), F32), jax.ShapeDtypeStruct((t, d), F32 if moe else BF16)]
    out_specs = [pl.BlockSpec((tm, d), lambda i: (i, 0)), pl.BlockSpec((tm, d), lambda i: (i, 0))]
    if moe:
        in_specs.append(pl.BlockSpec((d, 2 * LANES), lambda i: (0, 0)))
        args.append(wr)
        out_shape.append(jax.ShapeDtypeStruct((t, LANES), F32))
        out_specs.append(pl.BlockSpec((tm, LANES), lambda i: (i, 0)))
    return pl.pallas_call(
        functools.partial(_merge_kernel, moe=moe),
        out_shape=tuple(out_shape),
        grid=(t // tm,),
        in_specs=in_specs,
        out_specs=tuple(out_specs),
        compiler_params=_cparams("arbitrary"),
        name="merge_moe" if moe else "merge_dense",
    )(*args)


def _swiglu_partial(h, wg, wu, wd):
    g = _dot(h, wg)
    u = _dot(h, wu)
    return _dot((g * jax.nn.sigmoid(g) * u).astype(BF16), wd)


def _ffn_kernel(h_ref, wg_ref, wu_ref, wd_ref, x_ref, mod_ref, o_ref, acc_ref):
    j = pl.program_id(1)
    part = _swiglu_partial(h_ref[...], wg_ref[...], wu_ref[...], wd_ref[...])

    @pl.when(j == 0)
    def _():
        acc_ref[...] = part

    @pl.when(j > 0)
    def _():
        acc_ref[...] += part

    @pl.when(j == pl.num_programs(1) - 1)
    def _():
        o_ref[...] = x_ref[...] + mod_ref[0, 5:6, :] * acc_ref[...]


def _ff_chunk(d_ff):
    return 1408 if d_ff % 1408 == 0 else d_ff


def _ffn(h, wg, wu, wd, x2, mod, seq):
    t, d = x2.shape
    d_ff = wg.shape[1]
    tm = min(1024, seq)
    tf = _ff_chunk(d_ff)
    per_b = seq // tm
    return pl.pallas_call(
        _ffn_kernel,
        out_shape=jax.ShapeDtypeStruct((t, d), F32),
        grid=(t // tm, d_ff // tf),
        in_specs=[pl.BlockSpec((tm, d), lambda i, j: (i, 0)),
                  pl.BlockSpec((d, tf), lambda i, j: (0, j)),
                  pl.BlockSpec((d, tf), lambda i, j: (0, j)),
                  pl.BlockSpec((tf, d), lambda i, j: (j, 0)),
                  pl.BlockSpec((tm, d), lambda i, j: (i, 0)),
                  pl.BlockSpec((1, N_ADA, d), lambda i, j: (i // per_b, 0, 0))],
        out_specs=pl.BlockSpec((tm, d), lambda i, j: (i, 0)),
        scratch_shapes=[pltpu.VMEM((tm, d), F32)],
        compiler_params=_cparams("arbitrary", "arbitrary"),
        name="ffn_dense",
    )(h, wg, wu, wd, x2, mod)


def _gather_start(src_hbm, idx_ref, base, dst_ref, sem):
    for r in range(dst_ref.shape[0]):
        row = idx_ref[base + r]
        pltpu.make_async_copy(src_hbm.at[pl.ds(row, 1), :], dst_ref.at[pl.ds(r, 1), :], sem).start()


def _gather_wait(src_hbm, dst_ref, sem):
    pltpu.make_async_copy(src_hbm.at[pl.ds(0, dst_ref.shape[0]), :], dst_ref, sem).wait()


def _moe_kernel(te_ref, nu_ref, rt_ref, h_hbm, wg_ref, wu_ref, wd_ref, y_ref, xf_ref, xb_ref, acc_ref, sem):
    i = pl.program_id(0)
    j = pl.program_id(1)
    last = pl.num_programs(1) - 1
    n_tiles = pl.num_programs(0)
    tm = xb_ref.shape[0]
    valid = i < nu_ref[0]
    slot = i % 2

    @pl.when(j == 0)
    def _():
        @pl.when(i == 0)
        def _():
            _gather_start(h_hbm, rt_ref, 0, xf_ref.at[0], sem.at[0])

        _gather_wait(h_hbm, xf_ref.at[slot], sem.at[slot])
        xb_ref[...] = xf_ref[slot].astype(BF16)

        @pl.when(i + 1 < n_tiles)
        def _():
            _gather_start(h_hbm, rt_ref, (i + 1) * tm, xf_ref.at[1 - slot], sem.at[1 - slot])

    @pl.when(valid)
    def _():
        part = _swiglu_partial(xb_ref[...], wg_ref[0], wu_ref[0], wd_ref[0])

        @pl.when(j == 0)
        def _():
            acc_ref[...] = part

        @pl.when(j > 0)
        def _():
            acc_ref[...] += part

        @pl.when(j == last)
        def _():
            y_ref[...] = acc_ref[...]

    @pl.when(jnp.logical_and(jnp.logical_not(valid), j == last))
    def _():
        y_ref[...] = jnp.zeros_like(y_ref)


def _moe_experts(h, wg, wu, wd, tile_expert, n_used, row_token, tm):
    t, d = h.shape
    d_ff = wg.shape[2]
    tf = _ff_chunk(d_ff)
    nj = d_ff // tf
    n_tiles = tile_expert.shape[0]

    def chunk(i, j, nu):
        return jnp.where(i < nu[0], j, nj - 1)

    grid_spec = pltpu.PrefetchScalarGridSpec(
        num_scalar_prefetch=3,
        grid=(n_tiles, nj),
        in_specs=[pl.BlockSpec(memory_space=pl.ANY),
                  pl.BlockSpec((1, d, tf), lambda i, j, te, nu, rt: (te[i], 0, chunk(i, j, nu))),
                  pl.BlockSpec((1, d, tf), lambda i, j, te, nu, rt: (te[i], 0, chunk(i, j, nu))),
                  pl.BlockSpec((1, tf, d), lambda i, j, te, nu, rt: (te[i], chunk(i, j, nu), 0))],
        out_specs=pl.BlockSpec((tm, d), lambda i, j, te, nu, rt: (i, 0)),
        scratch_shapes=[pltpu.VMEM((2, tm, d), F32), pltpu.VMEM((tm, d), BF16), pltpu.VMEM((tm, d), F32),
                        pltpu.SemaphoreType.DMA((2,))],
    )
    return pl.pallas_call(
        _moe_kernel,
        out_shape=jax.ShapeDtypeStruct((n_tiles * tm, d), F32),
        grid_spec=grid_spec,
        compiler_params=_cparams("arbitrary", "arbitrary"),
        name="moe_experts",
    )(tile_expert, n_used, row_token, h, wg, wu, wd)


def _combine_kernel(slot_ref, y_hbm, route_ref, x_ref, mod_ref, o_ref, yb_ref, sem):
    i = pl.program_id(0)
    n = pl.num_programs(0)
    tm = x_ref.shape[0]
    rows = tm * TOP_K
    buf = i % 2

    @pl.when(i == 0)
    def _():
        _gather_start(y_hbm, slot_ref, 0, yb_ref.at[0], sem.at[0])

    @pl.when(i + 1 < n)
    def _():
        _gather_start(y_hbm, slot_ref, (i + 1) * rows, yb_ref.at[1 - buf], sem.at[1 - buf])

    _gather_wait(y_hbm, yb_ref.at[buf], sem.at[buf])
    route = route_ref[...]
    f = route[:, 2:3] * yb_ref[buf, 0:tm, :] + route[:, 3:4] * yb_ref[buf, tm:rows, :]
    o_ref[...] = x_ref[...] + mod_ref[0, 5:6, :] * f


def _combine(slot, y, route, x2, mod, seq):
    t, d = x2.shape
    tm = min(256, seq)
    per_b = seq // tm
    grid_spec = pltpu.PrefetchScalarGridSpec(
        num_scalar_prefetch=1,
        grid=(t // tm,),
        in_specs=[pl.BlockSpec(memory_space=pl.ANY),
                  pl.BlockSpec((tm, LANES), lambda i, sl: (i, 0)),
                  pl.BlockSpec((tm, d), lambda i, sl: (i, 0)),
                  pl.BlockSpec((1, N_ADA, d), lambda i, sl: (i // per_b, 0, 0))],
        out_specs=pl.BlockSpec((tm, d), lambda i, sl: (i, 0)),
        scratch_shapes=[pltpu.VMEM((2, TOP_K * tm, d), F32), pltpu.SemaphoreType.DMA((2,))],
    )
    slot_tiles = slot.reshape(t // tm, tm, TOP_K).transpose(0, 2, 1).reshape(-1)
    return pl.pallas_call(
        _combine_kernel,
        out_shape=jax.ShapeDtypeStruct((t, d), F32),
        grid_spec=grid_spec,
        compiler_params=_cparams("arbitrary"),
        name="moe_combine",
    )(slot_tiles, y, route, x2, mod)


def _routing_tables(route, tm):
    t = route.shape[0]
    p = t * TOP_K
    n_tiles = p // tm + N_EXPERTS
    e = route[:, :TOP_K].astype(jnp.int32).reshape(p)
    onehot = (e[:, None] == jnp.arange(N_EXPERTS, dtype=jnp.int32)[None, :]).astype(jnp.int32)
    csum = jnp.cumsum(onehot, axis=0)
    count = csum[-1]
    rank = jnp.take_along_axis(csum, e[:, None], axis=1)[:, 0] - 1
    tiles_e = (count + tm - 1) // tm
    tile_end = jnp.cumsum(tiles_e)
    tile_start = tile_end - tiles_e
    slot = tile_start[e] * tm + rank
    n_used = tile_end[-1]
    tile_id = jnp.minimum(jnp.arange(n_tiles, dtype=jnp.int32), n_used - 1)
    tile_expert = jnp.sum((tile_id[:, None] >= tile_end[None, :]).astype(jnp.int32), axis=1)
    row_token = jnp.zeros((n_tiles * tm,), jnp.int32).at[slot].set(
        jnp.arange(p, dtype=jnp.int32) // TOP_K, unique_indices=True)
    return tile_expert.astype(jnp.int32), n_used.reshape(1).astype(jnp.int32), row_token, slot.astype(jnp.int32)


def _pack_w_in(w):
    d = w.shape[0]
    sb_cols = 3 * SB_HEADS * HALF
    mla0 = sb_cols
    dif0 = mla0 + MLA_Q_RANK + MLA_KV_RANK + MLA_ROPE
    gate0 = dif0 + 3 * BRANCH_WIDTH
    z = lambda n: jnp.zeros((d, n), w.dtype)
    packed = jnp.concatenate([
        w[:, :sb_cols],
        w[:, mla0:mla0 + MLA_Q_RANK + MLA_KV_RANK],
        w[:, dif0:gate0],
        w[:, gate0:],
        z(HALF), w[:, mla0 + MLA_Q_RANK + MLA_KV_RANK:dif0], z(LANES - HALF - MLA_ROPE)], axis=1)
    assert packed.shape[1] == IN_COLS_PACKED
    return packed.astype(BF16)


def _pack_w_uq(w):
    r = w.shape[0]
    half = MLA_ROPE // 2
    w3 = w.reshape(r, MLA_HEADS, MLA_QK)
    z = lambda n: jnp.zeros((r, MLA_HEADS, n), w.dtype)
    plain = jnp.concatenate([w3, z(LANES - MLA_QK)], axis=2)
    swapped = jnp.concatenate([z(MLA_NOPE), w3[:, :, MLA_NOPE + half:], w3[:, :, MLA_NOPE:MLA_NOPE + half],
                               z(LANES - MLA_QK)], axis=2)
    return jnp.concatenate([plain.reshape(r, MLA_HEADS * LANES), swapped.reshape(r, MLA_HEADS * LANES)],
                           axis=1).astype(BF16)


def _pack_w_ukv(w):
    r = w.shape[0]
    w3 = w.reshape(r, MLA_HEADS, MLA_NOPE + HALF)
    wk = jnp.concatenate([w3[:, :, :MLA_NOPE], jnp.zeros((r, MLA_HEADS, LANES - MLA_NOPE), w.dtype)], axis=2)
    wv = w3[:, :, MLA_NOPE:]
    return wk.reshape(r, MLA_HEADS * LANES).astype(BF16), wv.reshape(r, MLA_HEADS * HALF).astype(BF16)


def _head_lane_gain(g):
    return jnp.concatenate([g, jnp.zeros((LANES - MLA_QK,), g.dtype)])


def _prep_table(mla_q_gain, mla_k_gain, diff_q_gain, diff_k_gain):
    half = MLA_ROPE // 2
    inv_freq = ROPE_THETA ** (-jnp.arange(half, dtype=F32) / half)
    zeros = lambda n: jnp.zeros((n,), F32)
    freq = jnp.concatenate([zeros(MLA_NOPE), inv_freq, inv_freq, zeros(LANES - MLA_QK)])
    sgn_lo = jnp.concatenate([zeros(MLA_NOPE), -jnp.ones((half,), F32), zeros(LANES - MLA_NOPE - half)])
    sgn_hi = jnp.concatenate([zeros(MLA_NOPE + half), jnp.ones((half,), F32), zeros(LANES - MLA_QK)])
    log2e = math.log2(math.e)
    rows = [freq, sgn_lo, sgn_hi,
            _head_lane_gain(mla_q_gain) * (log2e / math.sqrt(MLA_QK)),
            _head_lane_gain(mla_k_gain),
            jnp.tile(diff_q_gain, 2) * (log2e / math.sqrt(HALF)),
            jnp.tile(diff_k_gain, 2),
            zeros(LANES)]
    return jnp.stack(rows).astype(F32)


def kernel(x, c, positions, norm1_g, norm2_g, w_ada, b_ada, w_in, mla_q_norm, w_mla_uq, mla_kv_norm,
           w_mla_ukv, mla_q_gain, mla_k_gain, diff_q_gain, diff_k_gain, diff_lambda, diff_subln,
           w_branch, w_out, w_ffn_gate, w_ffn_up, w_ffn_down, w_router, w_exp_gate, w_exp_up, w_exp_down):
    batch, seq, d = x.shape
    depth = w_in.shape[0]
    t = batch * seq
    blk = min(256, seq)
    x2 = x.reshape(t, d)
    pos_col = positions.reshape(t, 1).astype(jnp.int32)
    pos_blk = positions.reshape(batch, seq // blk, 1, blk).astype(jnp.int32)
    slopes = 2.0 ** (-8.0 * jnp.arange(1, DIFF_HEADS + 1, dtype=F32) / DIFF_HEADS)
    mods = _ada(c, w_ada, b_ada).reshape(depth, batch, N_ADA, d)
    moe_tm = min(512, seq)

    for layer in range(depth):
        mod = mods[layer]
        proj = _in_proj(x2, norm1_g[layer].reshape(1, d), mod, _pack_w_in(w_in[layer]), seq)
        wuk, wuv = _pack_w_ukv(w_mla_ukv[layer])
        tab = _prep_table(mla_q_gain[layer], mla_k_gain[layer], diff_q_gain[layer], diff_k_gain[layer])
        qm, km, vm, dqn, dkn = _prep(proj, pos_col, tab, mla_q_norm[layer].reshape(1, -1),
                                     mla_kv_norm[layer].reshape(1, -1), _pack_w_uq(w_mla_uq[layer]),
                                     wuk, wuv, batch, seq)
        o_sb = _sb_attention(proj, batch, seq)
        o_mla = _mla_attention(qm, km, vm, batch, seq)
        lam_init = 0.8 - 0.6 * math.exp(-0.3 * layer)
        o_diff = _diff_attention(dqn, dkn, proj, pos_col, pos_blk, diff_lambda[layer],
                                 diff_subln[layer].reshape(1, LANES), slopes, lam_init, batch, seq)
        j = layer // 2
        moe = layer % 2 == 1
        wr = None
        if moe:
            wr = jnp.concatenate([w_router[j], jnp.zeros((d, LANES - N_EXPERTS), F32)], axis=1)
            wr_hi = wr.astype(BF16)
            wr = jnp.concatenate([wr_hi, (wr - wr_hi.astype(F32)).astype(BF16)], axis=1)
        outs = _merge(o_sb, o_mla, o_diff, proj, x2, mod, w_branch[layer].astype(BF16),
                      w_out[layer].astype(BF16), norm2_g[layer].reshape(1, d), wr, seq, moe)
        if moe:
            x_mid, h2, route = outs
            tile_expert, n_used, row_token, slot = _routing_tables(route, moe_tm)
            y = _moe_experts(h2, w_exp_gate[j].astype(BF16), w_exp_up[j].astype(BF16),
                             w_exp_down[j].astype(BF16), tile_expert, n_used, row_token, moe_tm)
            x2 = _combine(slot, y, route, x_mid, mod, seq)
        else:
            x_mid, h2 = outs
            x2 = _ffn(h2, w_ffn_gate[j].astype(BF16), w_ffn_up[j].astype(BF16),
                      w_ffn_down[j].astype(BF16), x_mid, mod, seq)
    return x2.reshape(batch, seq, d)
```

```python
import functools
import math

import jax
import jax.numpy as jnp
from jax import lax
from jax.experimental import pallas as pl
from jax.experimental.pallas import tpu as pltpu

F32 = jnp.float32
BF16 = jnp.bfloat16
HIGHEST = lax.Precision.HIGHEST

NORM_EPS = 1e-6
LANES = 128
HALF = 64

SB_HEADS = 8
MLA_HEADS = 8
MLA_Q_RANK = 768
MLA_KV_RANK = 256
MLA_NOPE = 64
MLA_ROPE = 32
MLA_QK = MLA_NOPE + MLA_ROPE
ROPE_THETA = 10000.0
DIFF_HEADS = 4
BRANCH_WIDTH = 512
N_BRANCH = 3
N_EXPERTS = 8
TOP_K = 2
N_ADA = 6
NEG_BIG = -1e30
ONES_ROWS = 16

COL_SB = 0
COL_CQ = 1536
COL_CKV = 2304
COL_DIFF = 2560
COL_GATE = 4096
COL_KROPE = 7168
IN_COLS_PACKED = 7296

VMEM_LIMIT = 56 * 1024 * 1024


def _cparams(*sem):
    return pltpu.CompilerParams(dimension_semantics=sem, vmem_limit_bytes=VMEM_LIMIT)


def _dot(a, b):
    return jnp.dot(a, b, preferred_element_type=F32)


def _dot_nt(a, b):
    return lax.dot_general(a, b, (((1,), (1,)), ((), ())), preferred_element_type=F32)


def _norm_mod(x, g, sc, sh):
    ms = jnp.mean(x * x, axis=-1, keepdims=True)
    return (x * lax.rsqrt(ms + NORM_EPS) * g) * (1.0 + sc) + sh


def _ada_kernel(c_ref, w_ref, b_ref, o_ref):
    c = c_ref[...]
    cond = c * jax.nn.sigmoid(c)
    o_ref[0] = jnp.dot(cond, w_ref[0], precision=HIGHEST, preferred_element_type=F32) + b_ref[0]


def _ada(c, w_ada, b_ada):
    depth, d, n = w_ada.shape
    b = c.shape[0]
    tn = 1536
    return pl.pallas_call(
        _ada_kernel,
        out_shape=jax.ShapeDtypeStruct((depth, b, n), F32),
        grid=(depth, n // tn),
        in_specs=[pl.BlockSpec((b, d), lambda l, j: (0, 0)),
                  pl.BlockSpec((1, d, tn), lambda l, j: (l, 0, j)),
                  pl.BlockSpec((1, 1, tn), lambda l, j: (l, 0, j))],
        out_specs=pl.BlockSpec((1, b, tn), lambda l, j: (l, 0, j)),
        compiler_params=_cparams("arbitrary", "arbitrary"),
        name="ada_mod",
    )(c, w_ada, b_ada.reshape(depth, 1, n))


def _in_proj_kernel(x_ref, g_ref, mod_ref, w_ref, o_ref):
    h = _norm_mod(x_ref[...], g_ref[...], mod_ref[0, 1:2, :], mod_ref[0, 0:1, :])
    o_ref[...] = _dot(h.astype(BF16), w_ref[...]).astype(o_ref.dtype)


def _in_proj(x2, g, mod, w, seq):
    t, d = x2.shape
    n = w.shape[1]
    tm = min(1024, seq)
    tn = 2432
    per_b = seq // tm
    return pl.pallas_call(
        _in_proj_kernel,
        out_shape=jax.ShapeDtypeStruct((t, n), BF16),
        grid=(t // tm, n // tn),
        in_specs=[pl.BlockSpec((tm, d), lambda i, j: (i, 0)),
                  pl.BlockSpec((1, d), lambda i, j: (0, 0)),
                  pl.BlockSpec((1, N_ADA, d), lambda i, j: (i // per_b, 0, 0)),
                  pl.BlockSpec((d, tn), lambda i, j: (0, j))],
        out_specs=pl.BlockSpec((tm, tn), lambda i, j: (i, j)),
        compiler_params=_cparams("arbitrary", "arbitrary"),
        name="in_proj",
    )(x2, g, mod, w)


def _prep_kernel(cq_ref, ckv_ref, kr_ref, dq_ref, dk_ref, pos_ref, tab_ref, qn_ref, kvn_ref,
                 wuq_ref, wuk_ref, wuv_ref, qm_ref, km_ref, vm_ref, dqn_ref, dkn_ref):
    tm = cq_ref.shape[0]
    lane = lax.broadcasted_iota(jnp.int32, (tm, LANES), 1)
    freq = tab_ref[0:1, :]
    sgn_lo = tab_ref[1:2, :]
    sgn_hi = tab_ref[2:3, :]
    gq = tab_ref[3:4, :]
    gk = tab_ref[4:5, :]
    gdq = tab_ref[5:6, :]
    gdk = tab_ref[6:7, :]

    ang = pos_ref[...].astype(F32) * freq
    cos = jnp.cos(ang)
    sin = jnp.sin(ang)
    s_lo = sin * sgn_lo
    s_hi = sin * sgn_hi
    s_pm = s_lo + s_hi

    def qk_norm(r, gain):
        ms = jnp.sum(r * r, axis=-1, keepdims=True) * (1.0 / MLA_QK)
        return (r * lax.rsqrt(ms + NORM_EPS) * gain).astype(BF16)

    cq = cq_ref[...].astype(F32)
    cqn = (cq * lax.rsqrt(jnp.mean(cq * cq, axis=-1, keepdims=True) + NORM_EPS) * qn_ref[...]).astype(BF16)
    q = _dot(cqn, wuq_ref[...])
    ckv = ckv_ref[...].astype(F32)
    ckvn = (ckv * lax.rsqrt(jnp.mean(ckv * ckv, axis=-1, keepdims=True) + NORM_EPS) * kvn_ref[...]).astype(BF16)
    kn = _dot(ckvn, wuk_ref[...])
    vm_ref[...] = _dot(ckvn, wuv_ref[...]).astype(BF16)
    kr = kr_ref[...].astype(F32)
    kr = kr * cos + pltpu.roll(kr, LANES - MLA_ROPE // 2, 1) * s_lo + pltpu.roll(kr, MLA_ROPE // 2, 1) * s_hi
    nh = MLA_HEADS
    for h in range(nh):
        qh = q[:, h * LANES:(h + 1) * LANES] * cos + q[:, (nh + h) * LANES:(nh + h + 1) * LANES] * s_pm
        qm_ref[0, h] = qk_norm(qh, gq)
        km_ref[0, h] = qk_norm(kn[:, h * LANES:(h + 1) * LANES] + kr, gk)

    def half_norm(ref, gain, out_ref):
        for cb in range(ref.shape[1] // LANES):
            v = ref[:, cb * LANES:(cb + 1) * LANES].astype(F32)
            sq = v * v
            lo = jnp.sum(jnp.where(lane < HALF, sq, 0.0), axis=-1, keepdims=True)
            hi = jnp.sum(jnp.where(lane >= HALF, sq, 0.0), axis=-1, keepdims=True)
            ms = jnp.where(lane < HALF, lo, hi) * (1.0 / HALF)
            out_ref[:, cb * LANES:(cb + 1) * LANES] = (v * lax.rsqrt(ms + NORM_EPS) * gain).astype(BF16)

    half_norm(dq_ref, gdq, dqn_ref)
    half_norm(dk_ref, gdk, dkn_ref)


def _prep(proj, pos_col, tab, qn, kvn, wuq, wuk, wuv, batch, seq):
    t = proj.shape[0]
    tm = min(512, seq)
    per_b = seq // tm
    h = MLA_HEADS
    const = lambda shape: pl.BlockSpec(shape, lambda i: (0,) * len(shape))
    return pl.pallas_call(
        _prep_kernel,
        out_shape=(jax.ShapeDtypeStruct((batch, h, seq, LANES), BF16),
                   jax.ShapeDtypeStruct((batch, h, seq, LANES), BF16),
                   jax.ShapeDtypeStruct((t, BRANCH_WIDTH), BF16),
                   jax.ShapeDtypeStruct((t, BRANCH_WIDTH), BF16),
                   jax.ShapeDtypeStruct((t, BRANCH_WIDTH), BF16)),
        grid=(t // tm,),
        in_specs=[pl.BlockSpec((tm, MLA_Q_RANK), lambda i: (i, COL_CQ // MLA_Q_RANK)),
                  pl.BlockSpec((tm, MLA_KV_RANK), lambda i: (i, COL_CKV // MLA_KV_RANK)),
                  pl.BlockSpec((tm, LANES), lambda i: (i, COL_KROPE // LANES)),
                  pl.BlockSpec((tm, BRANCH_WIDTH), lambda i: (i, COL_DIFF // BRANCH_WIDTH)),
                  pl.BlockSpec((tm, BRANCH_WIDTH), lambda i: (i, COL_DIFF // BRANCH_WIDTH + 1)),
                  pl.BlockSpec((tm, 1), lambda i: (i, 0)),
                  const((8, LANES)), const((1, MLA_Q_RANK)), const((1, MLA_KV_RANK)),
                  const((MLA_Q_RANK, 2 * h * LANES)), const((MLA_KV_RANK, h * LANES)),
                  const((MLA_KV_RANK, BRANCH_WIDTH))],
        out_specs=(pl.BlockSpec((1, h, tm, LANES), lambda i: (i // per_b, 0, i % per_b, 0)),
                   pl.BlockSpec((1, h, tm, LANES), lambda i: (i // per_b, 0, i % per_b, 0)),
                   pl.BlockSpec((tm, BRANCH_WIDTH), lambda i: (i, 0)),
                   pl.BlockSpec((tm, BRANCH_WIDTH), lambda i: (i, 0)),
                   pl.BlockSpec((tm, BRANCH_WIDTH), lambda i: (i, 0))),
        compiler_params=_cparams("arbitrary"),
        name="attn_prep",
    )(proj, proj, proj, proj, proj, pos_col, tab, qn, kvn, wuq, wuk, wuv)


def _causal_items(nq, backward):
    qb, kb, dg, first = [], [], [], []
    for qi in range(nq):
        order = range(qi, -1, -1) if backward else range(qi + 1)
        for n, j in enumerate(order):
            qb.append(qi), kb.append(j), dg.append(int(j == qi)), first.append(int(n == 0))
    as_arr = lambda v: jnp.asarray(v, dtype=jnp.int32)
    return as_arr(qb), as_arr(kb), as_arr(dg), as_arr(first)


PIPE_SLOTS = 4


def _run_pipeline(n, phases, carry):
    depth = len(phases)

    def iteration(u, carry, u_mod):
        for d in range(depth):
            carry = phases[d](u - d, (u_mod - d) % PIPE_SLOTS, carry)
        return carry

    for u in range(depth - 1):
        for d in range(u + 1):
            carry = phases[d](u - d, (u - d) % PIPE_SLOTS, carry)
    start = depth - 1
    for _ in range((n - start) % PIPE_SLOTS):
        carry = iteration(start, carry, start % PIPE_SLOTS)
        start += 1

    def body(t, carry):
        u = start + PIPE_SLOTS * t
        for r in range(PIPE_SLOTS):
            carry = iteration(u + r, carry, (start + r) % PIPE_SLOTS)
        return carry

    carry = lax.fori_loop(0, (n - start) // PIPE_SLOTS, body, carry)
    for u in range(n, n + depth - 1):
        for d in range(u - n + 1, depth):
            carry = phases[d](u - d, (u - d) % PIPE_SLOTS, carry)
    return carry


def _first_step():
    return jnp.logical_and(pl.program_id(0) == 0, pl.program_id(1) == 0)


def _sb_kernel(tq_ref, tk_ref, td_ref, tf_ref, q_ref, k_ref, v_ref, o_ref, q_s, acc_s, mbias_s, tri_s,
               *bufs, blk, nq, n_items):
    z_s, hl_s, w_s = bufs[0:4], bufs[4:8], bufs[8:12]
    lane = lax.broadcasted_iota(jnp.int32, (blk, LANES), 1)

    @pl.when(_first_step())
    def _():
        row = lax.broadcasted_iota(jnp.int32, (blk, blk), 0)
        col = lax.broadcasted_iota(jnp.int32, (blk, blk), 1)
        mbias_s[0] = jnp.zeros((blk, blk), F32)
        mbias_s[1] = jnp.where(col < row, 0.0, NEG_BIG)
        tri = jnp.where(row >= col, 1.0, 0.0).astype(BF16)
        tri_s[0:blk, :] = tri
        tri_s[blk:2 * blk, :] = tri

    scale = math.log2(math.e) / math.sqrt(HALF)
    for qi in range(nq):
        rows = slice(qi * blk, (qi + 1) * blk)
        q = (q_ref[rows, :].astype(F32) * scale).astype(BF16)
        q_s[0, rows, :] = jnp.where(lane < HALF, q, jnp.zeros_like(q))
        q_s[1, rows, :] = jnp.where(lane >= HALF, q, jnp.zeros_like(q))
    acc_s[...] = jnp.zeros(acc_s.shape, F32)

    def rows_of(block):
        return pl.ds(pl.multiple_of(block * blk, blk), blk)

    def scores(k, slot, c):
        par = slot % 2
        kb = k_ref[rows_of(tk_ref[k]), :]
        mbias = mbias_s[td_ref[k]]
        qrows = rows_of(tq_ref[k])
        for s in range(2):
            z = _dot_nt(q_s[s, qrows, :], kb) + mbias
            nz = -z
            sp = jnp.log2(1.0 + jnp.exp2(jnp.minimum(z, nz)))
            lneg = jnp.minimum(nz, 0.0) - sp
            hi = lneg.astype(BF16)
            hl_s[2 * par + s][:, 0:blk] = hi
            hl_s[2 * par + s][:, blk:2 * blk] = (lneg - hi.astype(F32)).astype(BF16)
            z_s[2 * par + s][...] = z
        return c

    def weights(k, slot, c):
        par = slot % 2
        diag = td_ref[k]
        out = []
        for s in range(2):
            incl = _dot(hl_s[2 * par + s][...], tri_s[...])
            cs = jnp.where(diag == 1, 0.0, c[s])
            w_s[2 * par + s][...] = jnp.exp2(z_s[2 * par + s][...] + incl + cs).astype(BF16)
            out.append(cs + incl[:, 0:1])
        return tuple(out)

    def values(k, slot, c):
        par = slot % 2
        vb = v_ref[rows_of(tk_ref[k]), :]
        qb = tq_ref[k]
        for s in range(2):
            acc_s[qb, s] += _dot(w_s[2 * par + s][...], vb)
        return c

    c = (jnp.zeros((blk, 1), F32), jnp.zeros((blk, 1), F32))
    _run_pipeline(n_items, (scores, weights, values), c)
    for qi in range(nq):
        o_ref[qi * blk:(qi + 1) * blk, :] = jnp.where(lane < HALF, acc_s[qi, 0], acc_s[qi, 1]).astype(o_ref.dtype)


def _sb_attention(proj, batch, seq):
    t = proj.shape[0]
    blk = min(256, seq)
    nq = seq // blk
    npair = SB_HEADS // 2
    qcol = COL_SB // LANES
    kcol = qcol + npair
    vcol = qcol + 2 * npair
    tables = _causal_items(nq, backward=True)
    grid_spec = pltpu.PrefetchScalarGridSpec(
        num_scalar_prefetch=4,
        grid=(batch, npair),
        in_specs=[pl.BlockSpec((seq, LANES), lambda b, p, *_: (b, qcol + p)),
                  pl.BlockSpec((seq, LANES), lambda b, p, *_: (b, kcol + p)),
                  pl.BlockSpec((seq, LANES), lambda b, p, *_: (b, vcol + p))],
        out_specs=pl.BlockSpec((seq, LANES), lambda b, p, *_: (b, p)),
        scratch_shapes=[pltpu.VMEM((2, seq, LANES), BF16),
                        pltpu.VMEM((nq, 2, blk, LANES), F32),
                        pltpu.VMEM((2, blk, blk), F32),
                        pltpu.VMEM((2 * blk, blk), BF16)]
        + [pltpu.VMEM((blk, blk), F32)] * 4
        + [pltpu.VMEM((blk, 2 * blk), BF16)] * 4
        + [pltpu.VMEM((blk, blk), BF16)] * 4,
    )
    return pl.pallas_call(
        functools.partial(_sb_kernel, blk=blk, nq=nq, n_items=int(tables[0].shape[0])),
        out_shape=jax.ShapeDtypeStruct((t, BRANCH_WIDTH), BF16),
        grid_spec=grid_spec,
        compiler_params=_cparams("arbitrary", "arbitrary"),
        name="sb_attention",
    )(*tables, proj, proj, proj)


def _init_causal_bias_t(mbias_s, blk):
    key = lax.broadcasted_iota(jnp.int32, (blk, blk), 0)
    qry = lax.broadcasted_iota(jnp.int32, (blk, blk), 1)
    mbias_s[0] = jnp.zeros((blk, blk), F32)
    mbias_s[1] = jnp.where(key <= qry, 0.0, NEG_BIG)


def _softmax_pipeline(tables, score_fn, vt_fn, acc_s, mbias_s, bufs, *, blk, n_items):
    tq_ref, tk_ref, td_ref, tf_ref = tables
    s_s, p_s, m_s, a_s, a2_s = bufs[0:4], bufs[4:8], bufs[8:12], bufs[12:16], bufs[16:20]

    def rows_of(block):
        return pl.ds(pl.multiple_of(block * blk, blk), blk)

    def scores(k, slot, carry):
        par = slot % 2
        first = tf_ref[k] == 1
        sc_pair = score_fn(k, rows_of(tq_ref[k]), rows_of(tk_ref[k]), mbias_s[td_ref[k]])
        m_out = []
        for s in range(2):
            m_prev = jnp.where(first, NEG_BIG, carry[s])
            sc = sc_pair[s]
            m_new = jnp.maximum(m_prev, jnp.max(sc, axis=0, keepdims=True))
            s_s[2 * par + s][...] = sc
            m_s[2 * par + s][...] = m_new
            a_s[2 * par + s][...] = jnp.exp2(m_prev - m_new)
            m_out.append(m_new)
        return tuple(m_out)

    def probabilities(k, slot, carry):
        par = slot % 2
        for s in range(2):
            p_s[2 * par + s][...] = jnp.exp2(s_s[2 * par + s][...] - m_s[2 * par + s][...]).astype(BF16)
            a2_s[2 * par + s][...] = a_s[2 * par + s][...]
        return carry

    def values(k, slot, carry):
        par = slot % 2
        kb = tk_ref[k]
        qb = tq_ref[k]
        for s in range(2):
            acc_s[qb, s] = a2_s[2 * par + s][...] * acc_s[qb, s] + _dot(vt_fn(s, kb), p_s[2 * par + s][...])
        return carry

    m0 = jnp.full((1, blk), NEG_BIG, F32)
    _run_pipeline(n_items, (scores, probabilities, values), (m0, m0))


def _softmax_scratch(blk, nq, acc_rows):
    return ([pltpu.VMEM((nq, 2, acc_rows, blk), F32), pltpu.VMEM((2, blk, blk), F32)]
            + [pltpu.VMEM((blk, blk), F32)] * 4 + [pltpu.VMEM((blk, blk), BF16)] * 4
            + [pltpu.VMEM((1, blk), F32)] * 12)


def _mla_kernel(tq_ref, tk_ref, td_ref, tf_ref, q_ref, k_ref, v_ref, o_ref, vt_s, acc_s, mbias_s, *bufs,
                blk, nq):
    vdim = lax.broadcasted_iota(jnp.int32, (LANES, blk), 0)

    @pl.when(_first_step())
    def _():
        _init_causal_bias_t(mbias_s, blk)

    acc_s[...] = jnp.zeros(acc_s.shape, F32)
    for kb in range(nq):
        vt = v_ref[kb * blk:(kb + 1) * blk, :].astype(F32).T
        vt_s[0, kb] = jnp.where(vdim < HALF, vt, 1.0).astype(BF16)
        vt_s[1, kb] = jnp.where(vdim >= HALF, vt, 1.0).astype(BF16)

    def score_fn(k, qrows, krows, mbias):
        return tuple(_dot_nt(k_ref[0, s, krows, :], q_ref[0, s, qrows, :]) + mbias for s in range(2))

    _softmax_pipeline((tq_ref, tk_ref, td_ref, tf_ref), score_fn, lambda s, kb: vt_s[s, kb], acc_s, mbias_s,
                      bufs, blk=blk, n_items=nq * (nq + 1) // 2)
    for qi in range(nq):
        a0 = acc_s[qi, 0]
        a1 = acc_s[qi, 1]
        o = jnp.where(vdim < HALF, a0 / a0[HALF:HALF + 1, :], a1 / a1[0:1, :])
        o_ref[qi * blk:(qi + 1) * blk, :] = o.T.astype(o_ref.dtype)


def _mla_attention(qm, km, vm, batch, seq):
    t = vm.shape[0]
    blk = min(256, seq)
    nq = seq // blk
    npair = MLA_HEADS // 2
    grid_spec = pltpu.PrefetchScalarGridSpec(
        num_scalar_prefetch=4,
        grid=(batch, npair),
        in_specs=[pl.BlockSpec((1, 2, seq, LANES), lambda b, p, *_: (b, p, 0, 0)),
                  pl.BlockSpec((1, 2, seq, LANES), lambda b, p, *_: (b, p, 0, 0)),
                  pl.BlockSpec((seq, LANES), lambda b, p, *_: (b, p))],
        out_specs=pl.BlockSpec((seq, LANES), lambda b, p, *_: (b, p)),
        scratch_shapes=[pltpu.VMEM((2, nq, LANES, blk), BF16)] + _softmax_scratch(blk, nq, LANES),
    )
    return pl.pallas_call(
        functools.partial(_mla_kernel, blk=blk, nq=nq),
        out_shape=jax.ShapeDtypeStruct((t, BRANCH_WIDTH), BF16),
        grid_spec=grid_spec,
        compiler_params=_cparams("arbitrary", "arbitrary"),
        name="mla_attention",
    )(*_causal_items(nq, backward=False), qm, km, vm)


def _diff_kernel(tq_ref, tk_ref, td_ref, tf_ref, slope_ref, q_ref, k_ref, v_ref, pcol_ref, prow_ref, lam_ref, g_ref,
                 o_ref, q_s, vt_s, pc_s, pr_s, acc_s, mbias_s, *bufs, blk, nq, lam_init):
    lane = lax.broadcasted_iota(jnp.int32, (blk, LANES), 1)

    @pl.when(_first_step())
    def _():
        _init_causal_bias_t(mbias_s, blk)

    acc_s[...] = jnp.zeros(acc_s.shape, F32)
    slope = slope_ref[pl.program_id(1)] * math.log2(math.e)
    pc_s[...] = pcol_ref[...].astype(F32) * slope
    pr_s[...] = prow_ref[0].astype(F32) * slope
    for qi in range(nq):
        rows = slice(qi * blk, (qi + 1) * blk)
        q = q_ref[rows, :]
        q_s[0, rows, :] = jnp.where(lane < HALF, q, jnp.zeros_like(q))
        q_s[1, rows, :] = jnp.where(lane >= HALF, q, jnp.zeros_like(q))
        vt_s[qi, 0:LANES, :] = v_ref[rows, :].astype(F32).T.astype(BF16)
        vt_s[qi, LANES:LANES + ONES_ROWS, :] = jnp.ones((ONES_ROWS, blk), BF16)

    def score_fn(k, qrows, krows, mbias):
        kb = k_ref[krows, :]
        bias = mbias - jnp.abs(pc_s[krows, :] - pr_s[tq_ref[k]])
        return tuple(_dot_nt(kb, q_s[s, qrows, :]) + bias for s in range(2))

    _softmax_pipeline((tq_ref, tk_ref, td_ref, tf_ref), score_fn, lambda s, kb: vt_s[kb], acc_s, mbias_s,
                      bufs, blk=blk, n_items=nq * (nq + 1) // 2)

    lp = lam_ref[...]
    e1 = jnp.exp(jnp.sum(lp[0:1, :] * lp[1:2, :], axis=-1, keepdims=True))
    e2 = jnp.exp(jnp.sum(lp[2:3, :] * lp[3:4, :], axis=-1, keepdims=True))
    lam = e1 - e2 + lam_init
    for qi in range(nq):
        a0 = acc_s[qi, 0]
        a1 = acc_s[qi, 1]
        o = (a0[0:LANES, :] / a0[LANES:LANES + 1, :] - lam * (a1[0:LANES, :] / a1[LANES:LANES + 1, :])).T
        ms = jnp.mean(o * o, axis=-1, keepdims=True)
        o_ref[qi * blk:(qi + 1) * blk, :] = (
            o * lax.rsqrt(ms + NORM_EPS) * g_ref[...] * (1.0 - lam_init)).astype(o_ref.dtype)


def _diff_attention(dqn, dkn, proj, pos_col, pos_blk, diff_lambda, subln, slopes, lam_init, batch, seq):
    t = dqn.shape[0]
    blk = min(256, seq)
    nq = seq // blk
    vcol = (COL_DIFF + 2 * BRANCH_WIDTH) // LANES
    grid_spec = pltpu.PrefetchScalarGridSpec(
        num_scalar_prefetch=5,
        grid=(batch, DIFF_HEADS),
        in_specs=[pl.BlockSpec((seq, LANES), lambda b, h, *_: (b, h)),
                  pl.BlockSpec((seq, LANES), lambda b, h, *_: (b, h)),
                  pl.BlockSpec((seq, LANES), lambda b, h, *_: (b, vcol + h)),
                  pl.BlockSpec((seq, 1), lambda b, h, *_: (b, 0)),
                  pl.BlockSpec((1, nq, 1, blk), lambda b, h, *_: (b, 0, 0, 0)),
                  pl.BlockSpec((4, HALF), lambda b, h, *_: (0, 0)),
                  pl.BlockSpec((1, LANES), lambda b, h, *_: (0, 0))],
        out_specs=pl.BlockSpec((seq, LANES), lambda b, h, *_: (b, h)),
        scratch_shapes=[pltpu.VMEM((2, seq, LANES), BF16), pltpu.VMEM((nq, LANES + ONES_ROWS, blk), BF16),
                        pltpu.VMEM((seq, 1), F32), pltpu.VMEM((nq, 1, blk), F32)]
        + _softmax_scratch(blk, nq, LANES + ONES_ROWS),
    )
    return pl.pallas_call(
        functools.partial(_diff_kernel, blk=blk, nq=nq, lam_init=lam_init),
        out_shape=jax.ShapeDtypeStruct((t, BRANCH_WIDTH), BF16),
        grid_spec=grid_spec,
        compiler_params=_cparams("arbitrary", "arbitrary"),
        name="diff_attention",
    )(*_causal_items(nq, backward=False), slopes, dqn, dkn, proj, pos_col, pos_blk, diff_lambda, subln)


def _merge_kernel(*refs, moe):
    (osb_ref, omla_ref, odiff_ref, g0_ref, g1_ref, g2_ref, x_ref, mod_ref, wb_ref, wo_ref, ng_ref) = refs[:11]
    if moe:
        wr_ref, xo_ref, h_ref, route_ref = refs[11:]
    else:
        xo_ref, h_ref = refs[11:]
    merged = None
    for n, (o_ref, g_ref) in enumerate(((osb_ref, g0_ref), (omla_ref, g1_ref), (odiff_ref, g2_ref))):
        y = jax.nn.sigmoid(g_ref[...].astype(F32)) * _dot(o_ref[...], wb_ref[n])
        merged = y if merged is None else merged + y
    mix = _dot(merged.astype(BF16), wo_ref[...])
    xn = x_ref[...] + mod_ref[0, 2:3, :] * mix
    xo_ref[...] = xn
    h = _norm_mod(xn, ng_ref[...], mod_ref[0, 4:5, :], mod_ref[0, 3:4, :])
    h_ref[...] = h.astype(h_ref.dtype)
    if moe:
        tm = h.shape[0]
        lane = lax.broadcasted_iota(jnp.int32, (tm, LANES), 1)
        lane_f = lane.astype(F32)
        h_hi = h.astype(BF16)
        h_lo = (h - h_hi.astype(F32)).astype(BF16)
        both = _dot(h_hi, wr_ref[...])
        logits = both[:, 0:LANES] + both[:, LANES:2 * LANES] + _dot(h_lo, wr_ref[:, 0:LANES])
        lg = jnp.where(lane < N_EXPERTS, logits, NEG_BIG)
        m1 = jnp.max(lg, axis=-1, keepdims=True)
        i1 = jnp.min(jnp.where(lg == m1, lane_f, float(LANES)), axis=-1, keepdims=True)
        lg2 = jnp.where(lane_f == i1, NEG_BIG, lg)
        m2 = jnp.max(lg2, axis=-1, keepdims=True)
        i2 = jnp.min(jnp.where(lg2 == m2, lane_f, float(LANES)), axis=-1, keepdims=True)
        e = jnp.exp(m2 - m1)
        w1 = 1.0 / (1.0 + e)
        w2 = e / (1.0 + e)
        route_ref[...] = jnp.where(lane == 0, i1, jnp.where(lane == 1, i2, jnp.where(
            lane == 2, w1, jnp.where(lane == 3, w2, 0.0))))


def _merge(o_sb, o_mla, o_diff, proj, x2, mod, wb, wo, ng, wr, seq, moe):
    t, d = x2.shape
    tm = min(512, seq)
    per_b = seq // tm
    gcol = COL_GATE // d
    in_specs = [pl.BlockSpec((tm, BRANCH_WIDTH), lambda i: (i, 0)),
                pl.BlockSpec((tm, BRANCH_WIDTH), lambda i: (i, 0)),
                pl.BlockSpec((tm, BRANCH_WIDTH), lambda i: (i, 0)),
                pl.BlockSpec((tm, d), lambda i: (i, gcol)),
                pl.BlockSpec((tm, d), lambda i: (i, gcol + 1)),
                pl.BlockSpec((tm, d), lambda i: (i, gcol + 2)),
                pl.BlockSpec((tm, d), lambda i: (i, 0)),
                pl.BlockSpec((1, N_ADA, d), lambda i: (i // per_b, 0, 0)),
                pl.BlockSpec((N_BRANCH, BRANCH_WIDTH, d), lambda i: (0, 0, 0)),
                pl.BlockSpec((d, d), lambda i: (0, 0)),
                pl.BlockSpec((1, d), lambda i: (0, 0))]
    args = [o_sb, o_mla, o_diff, proj, proj, proj, x2, mod, wb, wo, ng]
    out_shape = [jax.ShapeDtypeStruct((t, d), F32), jax.ShapeDtypeStruct((t, d), F32 if moe else BF16)]
    out_specs = [pl.BlockSpec((tm, d), lambda i: (i, 0)), pl.BlockSpec((tm, d), lambda i: (i, 0))]
    if moe:
        in_specs.append(pl.BlockSpec((d, 2 * LANES), lambda i: (0, 0)))
        args.append(wr)
        out_shape.append(jax.ShapeDtypeStruct((t, LANES), F32))
        out_specs.append(pl.BlockSpec((tm, LANES), lambda i: (i, 0)))
    return pl.pallas_call(
        functools.partial(_merge_kernel, moe=moe),
        out_shape=tuple(out_shape),
        grid=(t // tm,),
        in_specs=in_specs,
        out_specs=tuple(out_specs),
        compiler_params=_cparams("arbitrary"),
        name="merge_moe" if moe else "merge_dense",
    )(*args)


def _swiglu_partial(h, wg, wu, wd):
    g = _dot(h, wg)
    u = _dot(h, wu)
    return _dot((g * jax.nn.sigmoid(g) * u).astype(BF16), wd)


def _ffn_kernel(h_ref, wg_ref, wu_ref, wd_ref, x_ref, mod_ref, o_ref, acc_ref):
    j = pl.program_id(1)
    part = _swiglu_partial(h_ref[...], wg_ref[...], wu_ref[...], wd_ref[...])

    @pl.when(j == 0)
    def _():
        acc_ref[...] = part

    @pl.when(j > 0)
    def _():
        acc_ref[...] += part

    @pl.when(j == pl.num_programs(1) - 1)
    def _():
        o_ref[...] = x_ref[...] + mod_ref[0, 5:6, :] * acc_ref[...]


def _ffn(h, wg, wu, wd, x2, mod, seq):
    t, d = x2.shape
    d_ff = wg.shape[1]
    tm = min(512, seq)
    tf = d_ff
    per_b = seq // tm
    resident = pl.Buffered(1)
    return pl.pallas_call(
        _ffn_kernel,
        out_shape=jax.ShapeDtypeStruct((t, d), F32),
        grid=(t // tm, d_ff // tf),
        in_specs=[pl.BlockSpec((tm, d), lambda i, j: (i, 0)),
                  pl.BlockSpec((d, tf), lambda i, j: (0, j), pipeline_mode=resident),
                  pl.BlockSpec((d, tf), lambda i, j: (0, j), pipeline_mode=resident),
                  pl.BlockSpec((tf, d), lambda i, j: (j, 0), pipeline_mode=resident),
                  pl.BlockSpec((tm, d), lambda i, j: (i, 0)),
                  pl.BlockSpec((1, N_ADA, d), lambda i, j: (i // per_b, 0, 0))],
        out_specs=pl.BlockSpec((tm, d), lambda i, j: (i, 0)),
        scratch_shapes=[pltpu.VMEM((tm, d), F32)],
        compiler_params=_cparams("arbitrary", "arbitrary"),
        name="ffn_dense",
    )(h, wg, wu, wd, x2, mod)


def _gather_start(src_hbm, idx_ref, base, dst_ref, sem):
    for r in range(dst_ref.shape[0]):
        row = idx_ref[base + r]
        pltpu.make_async_copy(src_hbm.at[pl.ds(row, 1), :], dst_ref.at[pl.ds(r, 1), :], sem).start()


def _gather_wait(src_hbm, dst_ref, sem):
    pltpu.make_async_copy(src_hbm.at[pl.ds(0, dst_ref.shape[0]), :], dst_ref, sem).wait()


def _moe_kernel(te_ref, nu_ref, rt_ref, h_hbm, wg_ref, wu_ref, wd_ref, y_ref, xf_ref, xb_ref, acc_ref, sem):
    i = pl.program_id(0)
    j = pl.program_id(1)
    last = pl.num_programs(1) - 1
    n_tiles = pl.num_programs(0)
    tm = xb_ref.shape[0]
    valid = i < nu_ref[0]
    slot = i % 2

    @pl.when(j == 0)
    def _():
        @pl.when(i == 0)
        def _():
            _gather_start(h_hbm, rt_ref, 0, xf_ref.at[0], sem.at[0])

        _gather_wait(h_hbm, xf_ref.at[slot], sem.at[slot])
        xb_ref[...] = xf_ref[slot].astype(BF16)

        @pl.when(i + 1 < n_tiles)
        def _():
            _gather_start(h_hbm, rt_ref, (i + 1) * tm, xf_ref.at[1 - slot], sem.at[1 - slot])

    @pl.when(valid)
    def _():
        part = _swiglu_partial(xb_ref[...], wg_ref[0], wu_ref[0], wd_ref[0])

        @pl.when(j == 0)
        def _():
            acc_ref[...] = part

        @pl.when(j > 0)
        def _():
            acc_ref[...] += part

        @pl.when(j == last)
        def _():
            y_ref[...] = acc_ref[...]

    @pl.when(jnp.logical_and(jnp.logical_not(valid), j == last))
    def _():
        y_ref[...] = jnp.zeros_like(y_ref)


def _moe_experts(h, wg, wu, wd, tile_expert, n_used, row_token, tm):
    t, d = h.shape
    d_ff = wg.shape[2]
    tf = d_ff
    nj = d_ff // tf
    n_tiles = tile_expert.shape[0]
    resident = pl.Buffered(1)

    def chunk(i, j, nu):
        return jnp.where(i < nu[0], j, nj - 1)

    grid_spec = pltpu.PrefetchScalarGridSpec(
        num_scalar_prefetch=3,
        grid=(n_tiles, nj),
        in_specs=[pl.BlockSpec(memory_space=pl.ANY),
                  pl.BlockSpec((1, d, tf), lambda i, j, te, nu, rt: (te[i], 0, chunk(i, j, nu)),
                               pipeline_mode=resident),
                  pl.BlockSpec((1, d, tf), lambda i, j, te, nu, rt: (te[i], 0, chunk(i, j, nu)),
                               pipeline_mode=resident),
                  pl.BlockSpec((1, tf, d), lambda i, j, te, nu, rt: (te[i], chunk(i, j, nu), 0),
                               pipeline_mode=resident)],
        out_specs=pl.BlockSpec((tm, d), lambda i, j, te, nu, rt: (i, 0)),
        scratch_shapes=[pltpu.VMEM((2, tm, d), F32), pltpu.VMEM((tm, d), BF16), pltpu.VMEM((tm, d), F32),
                        pltpu.SemaphoreType.DMA((2,))],
    )
    return pl.pallas_call(
        _moe_kernel,
        out_shape=jax.ShapeDtypeStruct((n_tiles * tm, d), F32),
        grid_spec=grid_spec,
        compiler_params=_cparams("arbitrary", "arbitrary"),
        name="moe_experts",
    )(tile_expert, n_used, row_token, h, wg, wu, wd)


def _combine_kernel(slot_ref, y_hbm, route_ref, x_ref, mod_ref, o_ref, yb_ref, sem):
    i = pl.program_id(0)
    n = pl.num_programs(0)
    tm = x_ref.shape[0]
    rows = tm * TOP_K
    buf = i % 2

    @pl.when(i == 0)
    def _():
        _gather_start(y_hbm, slot_ref, 0, yb_ref.at[0], sem.at[0])

    @pl.when(i + 1 < n)
    def _():
        _gather_start(y_hbm, slot_ref, (i + 1) * rows, yb_ref.at[1 - buf], sem.at[1 - buf])

    _gather_wait(y_hbm, yb_ref.at[buf], sem.at[buf])
    route = route_ref[...]
    f = route[:, 2:3] * yb_ref[buf, 0:tm, :] + route[:, 3:4] * yb_ref[buf, tm:rows, :]
    o_ref[...] = x_ref[...] + mod_ref[0, 5:6, :] * f


def _combine(slot, y, route, x2, mod, seq):
    t, d = x2.shape
    tm = min(256, seq)
    per_b = seq // tm
    grid_spec = pltpu.PrefetchScalarGridSpec(
        num_scalar_prefetch=1,
        grid=(t // tm,),
        in_specs=[pl.BlockSpec(memory_space=pl.ANY),
                  pl.BlockSpec((tm, LANES), lambda i, sl: (i, 0)),
                  pl.BlockSpec((tm, d), lambda i, sl: (i, 0)),
                  pl.BlockSpec((1, N_ADA, d), lambda i, sl: (i // per_b, 0, 0))],
        out_specs=pl.BlockSpec((tm, d), lambda i, sl: (i, 0)),
        scratch_shapes=[pltpu.VMEM((2, TOP_K * tm, d), F32), pltpu.SemaphoreType.DMA((2,))],
    )
    slot_tiles = slot.reshape(t // tm, tm, TOP_K).transpose(0, 2, 1).reshape(-1)
    return pl.pallas_call(
        _combine_kernel,
        out_shape=jax.ShapeDtypeStruct((t, d), F32),
        grid_spec=grid_spec,
        compiler_params=_cparams("arbitrary"),
        name="moe_combine",
    )(slot_tiles, y, route, x2, mod)


def _routing_tables(route, tm):
    t = route.shape[0]
    p = t * TOP_K
    n_tiles = p // tm + N_EXPERTS
    e = route[:, :TOP_K].astype(jnp.int32).reshape(p)
    onehot = (e[:, None] == jnp.arange(N_EXPERTS, dtype=jnp.int32)[None, :]).astype(jnp.int32)
    csum = jnp.cumsum(onehot, axis=0)
    count = csum[-1]
    rank = jnp.take_along_axis(csum, e[:, None], axis=1)[:, 0] - 1
    tiles_e = (count + tm - 1) // tm
    tile_end = jnp.cumsum(tiles_e)
    tile_start = tile_end - tiles_e
    slot = tile_start[e] * tm + rank
    n_used = tile_end[-1]
    tile_id = jnp.minimum(jnp.arange(n_tiles, dtype=jnp.int32), n_used - 1)
    tile_expert = jnp.sum((tile_id[:, None] >= tile_end[None, :]).astype(jnp.int32), axis=1)
    row_token = jnp.zeros((n_tiles * tm,), jnp.int32).at[slot].set(
        jnp.arange(p, dtype=jnp.int32) // TOP_K, unique_indices=True)
    return tile_expert.astype(jnp.int32), n_used.reshape(1).astype(jnp.int32), row_token, slot.astype(jnp.int32)


def _pack_w_in(w):
    d = w.shape[0]
    sb_cols = 3 * SB_HEADS * HALF
    mla0 = sb_cols
    dif0 = mla0 + MLA_Q_RANK + MLA_KV_RANK + MLA_ROPE
    gate0 = dif0 + 3 * BRANCH_WIDTH
    z = lambda n: jnp.zeros((d, n), w.dtype)
    packed = jnp.concatenate([
        w[:, :sb_cols],
        w[:, mla0:mla0 + MLA_Q_RANK + MLA_KV_RANK],
        w[:, dif0:gate0],
        w[:, gate0:],
        z(HALF), w[:, mla0 + MLA_Q_RANK + MLA_KV_RANK:dif0], z(LANES - HALF - MLA_ROPE)], axis=1)
    assert packed.shape[1] == IN_COLS_PACKED
    return packed.astype(BF16)


def _pack_w_uq(w):
    r = w.shape[0]
    half = MLA_ROPE // 2
    w3 = w.reshape(r, MLA_HEADS, MLA_QK)
    z = lambda n: jnp.zeros((r, MLA_HEADS, n), w.dtype)
    plain = jnp.concatenate([w3, z(LANES - MLA_QK)], axis=2)
    swapped = jnp.concatenate([z(MLA_NOPE), w3[:, :, MLA_NOPE + half:], w3[:, :, MLA_NOPE:MLA_NOPE + half],
                               z(LANES - MLA_QK)], axis=2)
    return jnp.concatenate([plain.reshape(r, MLA_HEADS * LANES), swapped.reshape(r, MLA_HEADS * LANES)],
                           axis=1).astype(BF16)


def _pack_w_ukv(w):
    r = w.shape[0]
    w3 = w.reshape(r, MLA_HEADS, MLA_NOPE + HALF)
    wk = jnp.concatenate([w3[:, :, :MLA_NOPE], jnp.zeros((r, MLA_HEADS, LANES - MLA_NOPE), w.dtype)], axis=2)
    wv = w3[:, :, MLA_NOPE:]
    return wk.reshape(r, MLA_HEADS * LANES).astype(BF16), wv.reshape(r, MLA_HEADS * HALF).astype(BF16)


def _head_lane_gain(g):
    return jnp.concatenate([g, jnp.zeros((LANES - MLA_QK,), g.dtype)])


def _prep_table(mla_q_gain, mla_k_gain, diff_q_gain, diff_k_gain):
    half = MLA_ROPE // 2
    inv_freq = ROPE_THETA ** (-jnp.arange(half, dtype=F32) / half)
    zeros = lambda n: jnp.zeros((n,), F32)
    freq = jnp.concatenate([zeros(MLA_NOPE), inv_freq, inv_freq, zeros(LANES - MLA_QK)])
    sgn_lo = jnp.concatenate([zeros(MLA_NOPE), -jnp.ones((half,), F32), zeros(LANES - MLA_NOPE - half)])
    sgn_hi = jnp.concatenate([zeros(MLA_NOPE + half), jnp.ones((half,), F32), zeros(LANES - MLA_QK)])
    log2e = math.log2(math.e)
    rows = [freq, sgn_lo, sgn_hi,
            _head_lane_gain(mla_q_gain) * (log2e / math.sqrt(MLA_QK)),
            _head_lane_gain(mla_k_gain),
            jnp.tile(diff_q_gain, 2) * (log2e / math.sqrt(HALF)),
            jnp.tile(diff_k_gain, 2),
            zeros(LANES)]
    return jnp.stack(rows).astype(F32)


def kernel(x, c, positions, norm1_g, norm2_g, w_ada, b_ada, w_in, mla_q_norm, w_mla_uq, mla_kv_norm,
           w_mla_ukv, mla_q_gain, mla_k_gain, diff_q_gain, diff_k_gain, diff_lambda, diff_subln,
           w_branch, w_out, w_ffn_gate, w_ffn_up, w_ffn_down, w_router, w_exp_gate, w_exp_up, w_exp_down):
    batch, seq, d = x.shape
    depth = w_in.shape[0]
    t = batch * seq
    blk = min(256, seq)
    x2 = x.reshape(t, d)
    pos_col = positions.reshape(t, 1).astype(jnp.int32)
    pos_blk = positions.reshape(batch, seq // blk, 1, blk).astype(jnp.int32)
    slopes = 2.0 ** (-8.0 * jnp.arange(1, DIFF_HEADS + 1, dtype=F32) / DIFF_HEADS)
    mods = _ada(c, w_ada, b_ada).reshape(depth, batch, N_ADA, d)
    moe_tm = min(512, seq)

    for layer in range(depth):
        mod = mods[layer]
        proj = _in_proj(x2, norm1_g[layer].reshape(1, d), mod, _pack_w_in(w_in[layer]), seq)
        wuk, wuv = _pack_w_ukv(w_mla_ukv[layer])
        tab = _prep_table(mla_q_gain[layer], mla_k_gain[layer], diff_q_gain[layer], diff_k_gain[layer])
        qm, km, vm, dqn, dkn = _prep(proj, pos_col, tab, mla_q_norm[layer].reshape(1, -1),
                                     mla_kv_norm[layer].reshape(1, -1), _pack_w_uq(w_mla_uq[layer]),
                                     wuk, wuv, batch, seq)
        o_sb = _sb_attention(proj, batch, seq)
        o_mla = _mla_attention(qm, km, vm, batch, seq)
        lam_init = 0.8 - 0.6 * math.exp(-0.3 * layer)
        o_diff = _diff_attention(dqn, dkn, proj, pos_col, pos_blk, diff_lambda[layer],
                                 diff_subln[layer].reshape(1, LANES), slopes, lam_init, batch, seq)
        j = layer // 2
        moe = layer % 2 == 1
        wr = None
        if moe:
            wr = jnp.concatenate([w_router[j], jnp.zeros((d, LANES - N_EXPERTS), F32)], axis=1)
            wr_hi = wr.astype(BF16)
            wr = jnp.concatenate([wr_hi, (wr - wr_hi.astype(F32)).astype(BF16)], axis=1)
        outs = _merge(o_sb, o_mla, o_diff, proj, x2, mod, w_branch[layer].astype(BF16),
                      w_out[layer].astype(BF16), norm2_g[layer].reshape(1, d), wr, seq, moe)
        if moe:
            x_mid, h2, route = outs
            tile_expert, n_used, row_token, slot = _routing_tables(route, moe_tm)
            y = _moe_experts(h2, w_exp_gate[j].astype(BF16), w_exp_up[j].astype(BF16),
                             w_exp_down[j].astype(BF16), tile_expert, n_used, row_token, moe_tm)
            x2 = _combine(slot, y, route, x_mid, mod, seq)
        else:
            x_mid, h2 = outs
            x2 = _ffn(h2, w_ffn_gate[j].astype(BF16), w_ffn_up[j].astype(BF16),
                      w_ffn_down[j].astype(BF16), x_mid, mod, seq)
    return x2.reshape(batch, seq, d)
```

```python
import functools
import math

import jax
import jax.numpy as jnp
from jax import lax
from jax.experimental import pallas as pl
from jax.experimental.pallas import tpu as pltpu

F32 = jnp.float32
BF16 = jnp.bfloat16
HIGHEST = lax.Precision.HIGHEST

NORM_EPS = 1e-6
LANES = 128
HALF = 64

SB_HEADS = 8
MLA_HEADS = 8
MLA_Q_RANK = 768
MLA_KV_RANK = 256
MLA_NOPE = 64
MLA_ROPE = 32
MLA_QK = MLA_NOPE + MLA_ROPE
ROPE_THETA = 10000.0
DIFF_HEADS = 4
BRANCH_WIDTH = 512
N_BRANCH = 3
N_EXPERTS = 8
TOP_K = 2
N_ADA = 6
NEG_BIG = -1e30
ONES_ROWS = 16

COL_SB = 0
COL_CQ = 1536
COL_CKV = 2304
COL_DIFF = 2560
COL_GATE = 4096
COL_KROPE = 7168
IN_COLS_PACKED = 7296

VMEM_LIMIT = 56 * 1024 * 1024


def _cparams(*sem):
    return pltpu.CompilerParams(dimension_semantics=sem, vmem_limit_bytes=VMEM_LIMIT)


def _dot(a, b):
    return jnp.dot(a, b, preferred_element_type=F32)


def _dot_nt(a, b):
    return lax.dot_general(a, b, (((1,), (1,)), ((), ())), preferred_element_type=F32)


def _norm_mod(x, g, sc, sh):
    ms = jnp.mean(x * x, axis=-1, keepdims=True)
    return (x * lax.rsqrt(ms + NORM_EPS) * g) * (1.0 + sc) + sh


def _ada_kernel(c_ref, w_ref, b_ref, o_ref):
    c = c_ref[...]
    cond = c * jax.nn.sigmoid(c)
    o_ref[0] = jnp.dot(cond, w_ref[0], precision=HIGHEST, preferred_element_type=F32) + b_ref[0]


def _ada(c, w_ada, b_ada):
    depth, d, n = w_ada.shape
    b = c.shape[0]
    tn = 1536
    return pl.pallas_call(
        _ada_kernel,
        out_shape=jax.ShapeDtypeStruct((depth, b, n), F32),
        grid=(depth, n // tn),
        in_specs=[pl.BlockSpec((b, d), lambda l, j: (0, 0)),
                  pl.BlockSpec((1, d, tn), lambda l, j: (l, 0, j)),
                  pl.BlockSpec((1, 1, tn), lambda l, j: (l, 0, j))],
        out_specs=pl.BlockSpec((1, b, tn), lambda l, j: (l, 0, j)),
        compiler_params=_cparams("arbitrary", "arbitrary"),
        name="ada_mod",
    )(c, w_ada, b_ada.reshape(depth, 1, n))


def _in_proj_kernel(x_ref, g_ref, mod_ref, w_ref, o_ref):
    h = _norm_mod(x_ref[...], g_ref[...], mod_ref[0, 1:2, :], mod_ref[0, 0:1, :])
    o_ref[...] = _dot(h.astype(BF16), w_ref[...]).astype(o_ref.dtype)


def _in_proj(x2, g, mod, w, seq):
    t, d = x2.shape
    n = w.shape[1]
    tm = min(1024, seq)
    tn = 2432
    per_b = seq // tm
    return pl.pallas_call(
        _in_proj_kernel,
        out_shape=jax.ShapeDtypeStruct((t, n), BF16),
        grid=(t // tm, n // tn),
        in_specs=[pl.BlockSpec((tm, d), lambda i, j: (i, 0)),
                  pl.BlockSpec((1, d), lambda i, j: (0, 0)),
                  pl.BlockSpec((1, N_ADA, d), lambda i, j: (i // per_b, 0, 0)),
                  pl.BlockSpec((d, tn), lambda i, j: (0, j))],
        out_specs=pl.BlockSpec((tm, tn), lambda i, j: (i, j)),
        compiler_params=_cparams("arbitrary", "arbitrary"),
        name="in_proj",
    )(x2, g, mod, w)


def _prep_kernel(cq_ref, ckv_ref, kr_ref, dq_ref, dk_ref, pos_ref, tab_ref, qn_ref, kvn_ref,
                 wuq_ref, wuk_ref, wuv_ref, qm_ref, km_ref, vm_ref, dqn_ref, dkn_ref):
    tm = cq_ref.shape[0]
    lane = lax.broadcasted_iota(jnp.int32, (tm, LANES), 1)
    freq = tab_ref[0:1, :]
    sgn_lo = tab_ref[1:2, :]
    sgn_hi = tab_ref[2:3, :]
    gq = tab_ref[3:4, :]
    gk = tab_ref[4:5, :]
    gdq = tab_ref[5:6, :]
    gdk = tab_ref[6:7, :]

    ang = pos_ref[...].astype(F32) * freq
    cos = jnp.cos(ang)
    sin = jnp.sin(ang)
    s_lo = sin * sgn_lo
    s_hi = sin * sgn_hi
    s_pm = s_lo + s_hi

    def qk_norm(r, gain):
        ms = jnp.sum(r * r, axis=-1, keepdims=True) * (1.0 / MLA_QK)
        return (r * lax.rsqrt(ms + NORM_EPS) * gain).astype(BF16)

    cq = cq_ref[...].astype(F32)
    cqn = (cq * lax.rsqrt(jnp.mean(cq * cq, axis=-1, keepdims=True) + NORM_EPS) * qn_ref[...]).astype(BF16)
    q = _dot(cqn, wuq_ref[...])
    ckv = ckv_ref[...].astype(F32)
    ckvn = (ckv * lax.rsqrt(jnp.mean(ckv * ckv, axis=-1, keepdims=True) + NORM_EPS) * kvn_ref[...]).astype(BF16)
    kn = _dot(ckvn, wuk_ref[...])
    vm_ref[...] = _dot(ckvn, wuv_ref[...]).astype(BF16)
    kr = kr_ref[...].astype(F32)
    kr = kr * cos + pltpu.roll(kr, LANES - MLA_ROPE // 2, 1) * s_lo + pltpu.roll(kr, MLA_ROPE // 2, 1) * s_hi
    nh = MLA_HEADS
    for h in range(nh):
        qh = q[:, h * LANES:(h + 1) * LANES] * cos + q[:, (nh + h) * LANES:(nh + h + 1) * LANES] * s_pm
        qm_ref[0, h] = qk_norm(qh, gq)
        km_ref[0, h] = qk_norm(kn[:, h * LANES:(h + 1) * LANES] + kr, gk)

    def half_norm(ref, gain, out_ref):
        for cb in range(ref.shape[1] // LANES):
            v = ref[:, cb * LANES:(cb + 1) * LANES].astype(F32)
            sq = v * v
            lo = jnp.sum(jnp.where(lane < HALF, sq, 0.0), axis=-1, keepdims=True)
            hi = jnp.sum(jnp.where(lane >= HALF, sq, 0.0), axis=-1, keepdims=True)
            ms = jnp.where(lane < HALF, lo, hi) * (1.0 / HALF)
            out_ref[:, cb * LANES:(cb + 1) * LANES] = (v * lax.rsqrt(ms + NORM_EPS) * gain).astype(BF16)

    half_norm(dq_ref, gdq, dqn_ref)
    half_norm(dk_ref, gdk, dkn_ref)


def _prep(proj, pos_col, tab, qn, kvn, wuq, wuk, wuv, batch, seq):
    t = proj.shape[0]
    tm = min(512, seq)
    per_b = seq // tm
    h = MLA_HEADS
    const = lambda shape: pl.BlockSpec(shape, lambda i: (0,) * len(shape))
    return pl.pallas_call(
        _prep_kernel,
        out_shape=(jax.ShapeDtypeStruct((batch, h, seq, LANES), BF16),
                   jax.ShapeDtypeStruct((batch, h, seq, LANES), BF16),
                   jax.ShapeDtypeStruct((t, BRANCH_WIDTH), BF16),
                   jax.ShapeDtypeStruct((t, BRANCH_WIDTH), BF16),
                   jax.ShapeDtypeStruct((t, BRANCH_WIDTH), BF16)),
        grid=(t // tm,),
        in_specs=[pl.BlockSpec((tm, MLA_Q_RANK), lambda i: (i, COL_CQ // MLA_Q_RANK)),
                  pl.BlockSpec((tm, MLA_KV_RANK), lambda i: (i, COL_CKV // MLA_KV_RANK)),
                  pl.BlockSpec((tm, LANES), lambda i: (i, COL_KROPE // LANES)),
                  pl.BlockSpec((tm, BRANCH_WIDTH), lambda i: (i, COL_DIFF // BRANCH_WIDTH)),
                  pl.BlockSpec((tm, BRANCH_WIDTH), lambda i: (i, COL_DIFF // BRANCH_WIDTH + 1)),
                  pl.BlockSpec((tm, 1), lambda i: (i, 0)),
                  const((8, LANES)), const((1, MLA_Q_RANK)), const((1, MLA_KV_RANK)),
                  const((MLA_Q_RANK, 2 * h * LANES)), const((MLA_KV_RANK, h * LANES)),
                  const((MLA_KV_RANK, BRANCH_WIDTH))],
        out_specs=(pl.BlockSpec((1, h, tm, LANES), lambda i: (i // per_b, 0, i % per_b, 0)),
                   pl.BlockSpec((1, h, tm, LANES), lambda i: (i // per_b, 0, i % per_b, 0)),
                   pl.BlockSpec((tm, BRANCH_WIDTH), lambda i: (i, 0)),
                   pl.BlockSpec((tm, BRANCH_WIDTH), lambda i: (i, 0)),
                   pl.BlockSpec((tm, BRANCH_WIDTH), lambda i: (i, 0))),
        compiler_params=_cparams("arbitrary"),
        name="attn_prep",
    )(proj, proj, proj, proj, proj, pos_col, tab, qn, kvn, wuq, wuk, wuv)


def _causal_items(nq, backward):
    qb, kb, dg, first = [], [], [], []
    for qi in range(nq):
        order = range(qi, -1, -1) if backward else range(qi + 1)
        for n, j in enumerate(order):
            qb.append(qi), kb.append(j), dg.append(int(j == qi)), first.append(int(n == 0))
    as_arr = lambda v: jnp.asarray(v, dtype=jnp.int32)
    return as_arr(qb), as_arr(kb), as_arr(dg), as_arr(first)


PIPE_SLOTS = 4


def _run_pipeline(n, phases, carry):
    depth = len(phases)

    def iteration(u, carry, u_mod):
        for d in range(depth):
            carry = phases[d](u - d, (u_mod - d) % PIPE_SLOTS, carry)
        return carry

    for u in range(depth - 1):
        for d in range(u + 1):
            carry = phases[d](u - d, (u - d) % PIPE_SLOTS, carry)
    start = depth - 1
    for _ in range((n - start) % PIPE_SLOTS):
        carry = iteration(start, carry, start % PIPE_SLOTS)
        start += 1

    def body(t, carry):
        u = start + PIPE_SLOTS * t
        for r in range(PIPE_SLOTS):
            carry = iteration(u + r, carry, (start + r) % PIPE_SLOTS)
        return carry

    carry = lax.fori_loop(0, (n - start) // PIPE_SLOTS, body, carry)
    for u in range(n, n + depth - 1):
        for d in range(u - n + 1, depth):
            carry = phases[d](u - d, (u - d) % PIPE_SLOTS, carry)
    return carry


def _first_step():
    return jnp.logical_and(pl.program_id(0) == 0, pl.program_id(1) == 0)


def _sb_kernel(tq_ref, tk_ref, td_ref, tf_ref, q_ref, k_ref, v_ref, o_ref, q_s, acc_s, mbias_s, tri_s,
               *bufs, blk, nq, n_items):
    z_s, hl_s, w_s = bufs[0:4], bufs[4:8], bufs[8:12]
    lane = lax.broadcasted_iota(jnp.int32, (blk, LANES), 1)

    @pl.when(_first_step())
    def _():
        row = lax.broadcasted_iota(jnp.int32, (blk, blk), 0)
        col = lax.broadcasted_iota(jnp.int32, (blk, blk), 1)
        mbias_s[0] = jnp.zeros((blk, blk), F32)
        mbias_s[1] = jnp.where(col < row, 0.0, NEG_BIG)
        tri = jnp.where(row >= col, 1.0, 0.0).astype(BF16)
        tri_s[0:blk, :] = tri
        tri_s[blk:2 * blk, :] = tri

    scale = math.log2(math.e) / math.sqrt(HALF)
    for qi in range(nq):
        rows = slice(qi * blk, (qi + 1) * blk)
        q = (q_ref[rows, :].astype(F32) * scale).astype(BF16)
        q_s[0, rows, :] = jnp.where(lane < HALF, q, jnp.zeros_like(q))
        q_s[1, rows, :] = jnp.where(lane >= HALF, q, jnp.zeros_like(q))
    acc_s[...] = jnp.zeros(acc_s.shape, F32)

    def rows_of(block):
        return pl.ds(pl.multiple_of(block * blk, blk), blk)

    def scores(k, slot, c):
        par = slot % 2
        kb = k_ref[rows_of(tk_ref[k]), :]
        mbias = mbias_s[td_ref[k]]
        qrows = rows_of(tq_ref[k])
        for s in range(2):
            z = _dot_nt(q_s[s, qrows, :], kb) + mbias
            nz = -z
            sp = jnp.log2(1.0 + jnp.exp2(jnp.minimum(z, nz)))
            lneg = jnp.minimum(nz, 0.0) - sp
            hi = lneg.astype(BF16)
            hl_s[2 * par + s][:, 0:blk] = hi
            hl_s[2 * par + s][:, blk:2 * blk] = (lneg - hi.astype(F32)).astype(BF16)
            z_s[2 * par + s][...] = z
        return c

    def weights(k, slot, c):
        par = slot % 2
        diag = td_ref[k]
        out = []
        for s in range(2):
            incl = _dot(hl_s[2 * par + s][...], tri_s[...])
            cs = jnp.where(diag == 1, 0.0, c[s])
            w_s[2 * par + s][...] = jnp.exp2(z_s[2 * par + s][...] + incl + cs).astype(BF16)
            out.append(cs + incl[:, 0:1])
        return tuple(out)

    def values(k, slot, c):
        par = slot % 2
        vb = v_ref[rows_of(tk_ref[k]), :]
        qb = tq_ref[k]
        for s in range(2):
            acc_s[qb, s] += _dot(w_s[2 * par + s][...], vb)
        return c

    c = (jnp.zeros((blk, 1), F32), jnp.zeros((blk, 1), F32))
    _run_pipeline(n_items, (scores, weights, values), c)
    for qi in range(nq):
        o_ref[qi * blk:(qi + 1) * blk, :] = jnp.where(lane < HALF, acc_s[qi, 0], acc_s[qi, 1]).astype(o_ref.dtype)


def _sb_attention(proj, batch, seq):
    t = proj.shape[0]
    blk = min(256, seq)
    nq = seq // blk
    npair = SB_HEADS // 2
    qcol = COL_SB // LANES
    kcol = qcol + npair
    vcol = qcol + 2 * npair
    tables = _causal_items(nq, backward=True)
    grid_spec = pltpu.PrefetchScalarGridSpec(
        num_scalar_prefetch=4,
        grid=(batch, npair),
        in_specs=[pl.BlockSpec((seq, LANES), lambda b, p, *_: (b, qcol + p)),
                  pl.BlockSpec((seq, LANES), lambda b, p, *_: (b, kcol + p)),
                  pl.BlockSpec((seq, LANES), lambda b, p, *_: (b, vcol + p))],
        out_specs=pl.BlockSpec((seq, LANES), lambda b, p, *_: (b, p)),
        scratch_shapes=[pltpu.VMEM((2, seq, LANES), BF16),
                        pltpu.VMEM((nq, 2, blk, LANES), F32),
                        pltpu.VMEM((2, blk, blk), F32),
                        pltpu.VMEM((2 * blk, blk), BF16)]
        + [pltpu.VMEM((blk, blk), F32)] * 4
        + [pltpu.VMEM((blk, 2 * blk), BF16)] * 4
        + [pltpu.VMEM((blk, blk), BF16)] * 4,
    )
    return pl.pallas_call(
        functools.partial(_sb_kernel, blk=blk, nq=nq, n_items=int(tables[0].shape[0])),
        out_shape=jax.ShapeDtypeStruct((t, BRANCH_WIDTH), BF16),
        grid_spec=grid_spec,
        compiler_params=_cparams("arbitrary", "arbitrary"),
        name="sb_attention",
    )(*tables, proj, proj, proj)


def _init_causal_bias_t(mbias_s, blk):
    key = lax.broadcasted_iota(jnp.int32, (blk, blk), 0)
    qry = lax.broadcasted_iota(jnp.int32, (blk, blk), 1)
    mbias_s[0] = jnp.zeros((blk, blk), F32)
    mbias_s[1] = jnp.where(key <= qry, 0.0, NEG_BIG)


def _softmax_pipeline(tables, score_fn, vt_fn, acc_s, mbias_s, bufs, *, blk, n_items):
    tq_ref, tk_ref, td_ref, tf_ref = tables
    s_s, p_s, m_s, a_s, a2_s = bufs[0:4], bufs[4:8], bufs[8:12], bufs[12:16], bufs[16:20]

    def rows_of(block):
        return pl.ds(pl.multiple_of(block * blk, blk), blk)

    def scores(k, slot, carry):
        par = slot % 2
        first = tf_ref[k] == 1
        sc_pair = score_fn(k, rows_of(tq_ref[k]), rows_of(tk_ref[k]), mbias_s[td_ref[k]])
        m_out = []
        for s in range(2):
            m_prev = jnp.where(first, NEG_BIG, carry[s])
            sc = sc_pair[s]
            m_new = jnp.maximum(m_prev, jnp.max(sc, axis=0, keepdims=True))
            s_s[2 * par + s][...] = sc
            m_s[2 * par + s][...] = m_new
            a_s[2 * par + s][...] = jnp.exp2(m_prev - m_new)
            m_out.append(m_new)
        return tuple(m_out)

    def probabilities(k, slot, carry):
        par = slot % 2
        for s in range(2):
            p_s[2 * par + s][...] = jnp.exp2(s_s[2 * par + s][...] - m_s[2 * par + s][...]).astype(BF16)
            a2_s[2 * par + s][...] = a_s[2 * par + s][...]
        return carry

    def values(k, slot, carry):
        par = slot % 2
        kb = tk_ref[k]
        qb = tq_ref[k]
        for s in range(2):
            acc_s[qb, s] = a2_s[2 * par + s][...] * acc_s[qb, s] + _dot(vt_fn(s, kb), p_s[2 * par + s][...])
        return carry

    m0 = jnp.full((1, blk), NEG_BIG, F32)
    _run_pipeline(n_items, (scores, probabilities, values), (m0, m0))


def _softmax_scratch(blk, nq, acc_rows):
    return ([pltpu.VMEM((nq, 2, acc_rows, blk), F32), pltpu.VMEM((2, blk, blk), F32)]
            + [pltpu.VMEM((blk, blk), F32)] * 4 + [pltpu.VMEM((blk, blk), BF16)] * 4
            + [pltpu.VMEM((1, blk), F32)] * 12)


def _mla_kernel(tq_ref, tk_ref, td_ref, tf_ref, q_ref, k_ref, v_ref, o_ref, vt_s, acc_s, mbias_s, *bufs,
                blk, nq):
    vdim = lax.broadcasted_iota(jnp.int32, (LANES, blk), 0)

    @pl.when(_first_step())
    def _():
        _init_causal_bias_t(mbias_s, blk)

    acc_s[...] = jnp.zeros(acc_s.shape, F32)
    for kb in range(nq):
        vt = v_ref[kb * blk:(kb + 1) * blk, :].astype(F32).T
        vt_s[0, kb] = jnp.where(vdim < HALF, vt, 1.0).astype(BF16)
        vt_s[1, kb] = jnp.where(vdim >= HALF, vt, 1.0).astype(BF16)

    def score_fn(k, qrows, krows, mbias):
        return tuple(_dot_nt(k_ref[0, s, krows, :], q_ref[0, s, qrows, :]) + mbias for s in range(2))

    _softmax_pipeline((tq_ref, tk_ref, td_ref, tf_ref), score_fn, lambda s, kb: vt_s[s, kb], acc_s, mbias_s,
                      bufs, blk=blk, n_items=nq * (nq + 1) // 2)
    for qi in range(nq):
        a0 = acc_s[qi, 0]
        a1 = acc_s[qi, 1]
        o = jnp.where(vdim < HALF, a0 / a0[HALF:HALF + 1, :], a1 / a1[0:1, :])
        o_ref[qi * blk:(qi + 1) * blk, :] = o.T.astype(o_ref.dtype)


def _mla_attention(qm, km, vm, batch, seq):
    t = vm.shape[0]
    blk = min(256, seq)
    nq = seq // blk
    npair = MLA_HEADS // 2
    grid_spec = pltpu.PrefetchScalarGridSpec(
        num_scalar_prefetch=4,
        grid=(batch, npair),
        in_specs=[pl.BlockSpec((1, 2, seq, LANES), lambda b, p, *_: (b, p, 0, 0)),
                  pl.BlockSpec((1, 2, seq, LANES), lambda b, p, *_: (b, p, 0, 0)),
                  pl.BlockSpec((seq, LANES), lambda b, p, *_: (b, p))],
        out_specs=pl.BlockSpec((seq, LANES), lambda b, p, *_: (b, p)),
        scratch_shapes=[pltpu.VMEM((2, nq, LANES, blk), BF16)] + _softmax_scratch(blk, nq, LANES),
    )
    return pl.pallas_call(
        functools.partial(_mla_kernel, blk=blk, nq=nq),
        out_shape=jax.ShapeDtypeStruct((t, BRANCH_WIDTH), BF16),
        grid_spec=grid_spec,
        compiler_params=_cparams("arbitrary", "arbitrary"),
        name="mla_attention",
    )(*_causal_items(nq, backward=False), qm, km, vm)


def _diff_kernel(tq_ref, tk_ref, td_ref, tf_ref, slope_ref, q_ref, k_ref, v_ref, pcol_ref, prow_ref, lam_ref, g_ref,
                 o_ref, q_s, vt_s, pc_s, pr_s, acc_s, mbias_s, *bufs, blk, nq, lam_init):
    lane = lax.broadcasted_iota(jnp.int32, (blk, LANES), 1)

    @pl.when(_first_step())
    def _():
        _init_causal_bias_t(mbias_s, blk)

    acc_s[...] = jnp.zeros(acc_s.shape, F32)
    slope = slope_ref[pl.program_id(1)] * math.log2(math.e)
    pc_s[...] = pcol_ref[...].astype(F32) * slope
    pr_s[...] = prow_ref[0].astype(F32) * slope
    for qi in range(nq):
        rows = slice(qi * blk, (qi + 1) * blk)
        q = q_ref[rows, :]
        q_s[0, rows, :] = jnp.where(lane < HALF, q, jnp.zeros_like(q))
        q_s[1, rows, :] = jnp.where(lane >= HALF, q, jnp.zeros_like(q))
        vt_s[qi, 0:LANES, :] = v_ref[rows, :].astype(F32).T.astype(BF16)
        vt_s[qi, LANES:LANES + ONES_ROWS, :] = jnp.ones((ONES_ROWS, blk), BF16)

    def score_fn(k, qrows, krows, mbias):
        kb = k_ref[krows, :]
        bias = mbias - jnp.abs(pc_s[krows, :] - pr_s[tq_ref[k]])
        return tuple(_dot_nt(kb, q_s[s, qrows, :]) + bias for s in range(2))

    _softmax_pipeline((tq_ref, tk_ref, td_ref, tf_ref), score_fn, lambda s, kb: vt_s[kb], acc_s, mbias_s,
                      bufs, blk=blk, n_items=nq * (nq + 1) // 2)

    lp = lam_ref[...]
    e1 = jnp.exp(jnp.sum(lp[0:1, :] * lp[1:2, :], axis=-1, keepdims=True))
    e2 = jnp.exp(jnp.sum(lp[2:3, :] * lp[3:4, :], axis=-1, keepdims=True))
    lam = e1 - e2 + lam_init
    for qi in range(nq):
        a0 = acc_s[qi, 0]
        a1 = acc_s[qi, 1]
        o = (a0[0:LANES, :] / a0[LANES:LANES + 1, :] - lam * (a1[0:LANES, :] / a1[LANES:LANES + 1, :])).T
        ms = jnp.mean(o * o, axis=-1, keepdims=True)
        o_ref[qi * blk:(qi + 1) * blk, :] = (
            o * lax.rsqrt(ms + NORM_EPS) * g_ref[...] * (1.0 - lam_init)).astype(o_ref.dtype)


def _diff_attention(dqn, dkn, proj, pos_col, pos_blk, diff_lambda, subln, slopes, lam_init, batch, seq):
    t = dqn.shape[0]
    blk = min(256, seq)
    nq = seq // blk
    vcol = (COL_DIFF + 2 * BRANCH_WIDTH) // LANES
    grid_spec = pltpu.PrefetchScalarGridSpec(
        num_scalar_prefetch=5,
        grid=(batch, DIFF_HEADS),
        in_specs=[pl.BlockSpec((seq, LANES), lambda b, h, *_: (b, h)),
                  pl.BlockSpec((seq, LANES), lambda b, h, *_: (b, h)),
                  pl.BlockSpec((seq, LANES), lambda b, h, *_: (b, vcol + h)),
                  pl.BlockSpec((seq, 1), lambda b, h, *_: (b, 0)),
                  pl.BlockSpec((1, nq, 1, blk), lambda b, h, *_: (b, 0, 0, 0)),
                  pl.BlockSpec((4, HALF), lambda b, h, *_: (0, 0)),
                  pl.BlockSpec((1, LANES), lambda b, h, *_: (0, 0))],
        out_specs=pl.BlockSpec((seq, LANES), lambda b, h, *_: (b, h)),
        scratch_shapes=[pltpu.VMEM((2, seq, LANES), BF16), pltpu.VMEM((nq, LANES + ONES_ROWS, blk), BF16),
                        pltpu.VMEM((seq, 1), F32), pltpu.VMEM((nq, 1, blk), F32)]
        + _softmax_scratch(blk, nq, LANES + ONES_ROWS),
    )
    return pl.pallas_call(
        functools.partial(_diff_kernel, blk=blk, nq=nq, lam_init=lam_init),
        out_shape=jax.ShapeDtypeStruct((t, BRANCH_WIDTH), BF16),
        grid_spec=grid_spec,
        compiler_params=_cparams("arbitrary", "arbitrary"),
        name="diff_attention",
    )(*_causal_items(nq, backward=False), slopes, dqn, dkn, proj, pos_col, pos_blk, diff_lambda, subln)


def _merge_kernel(*refs, moe):
    (osb_ref, omla_ref, odiff_ref, g0_ref, g1_ref, g2_ref, x_ref, mod_ref, wb_ref, wo_ref, ng_ref) = refs[:11]
    if moe:
        wr_ref, xo_ref, h_ref, route_ref = refs[11:]
    else:
        xo_ref, h_ref = refs[11:]
    merged = None
    for n, (o_ref, g_ref) in enumerate(((osb_ref, g0_ref), (omla_ref, g1_ref), (odiff_ref, g2_ref))):
        y = jax.nn.sigmoid(g_ref[...].astype(F32)) * _dot(o_ref[...], wb_ref[n])
        merged = y if merged is None else merged + y
    mix = _dot(merged.astype(BF16), wo_ref[...])
    xn = x_ref[...] + mod_ref[0, 2:3, :] * mix
    xo_ref[...] = xn
    h = _norm_mod(xn, ng_ref[...], mod_ref[0, 4:5, :], mod_ref[0, 3:4, :])
    h_ref[...] = h.astype(h_ref.dtype)
    if moe:
        tm = h.shape[0]
        lane = lax.broadcasted_iota(jnp.int32, (tm, LANES), 1)
        lane_f = lane.astype(F32)
        h_hi = h.astype(BF16)
        h_lo = (h - h_hi.astype(F32)).astype(BF16)
        both = _dot(h_hi, wr_ref[...])
        logits = both[:, 0:LANES] + both[:, LANES:2 * LANES] + _dot(h_lo, wr_ref[:, 0:LANES])
        lg = jnp.where(lane < N_EXPERTS, logits, NEG_BIG)
        m1 = jnp.max(lg, axis=-1, keepdims=True)
        i1 = jnp.min(jnp.where(lg == m1, lane_f, float(LANES)), axis=-1, keepdims=True)
        lg2 = jnp.where(lane_f == i1, NEG_BIG, lg)
        m2 = jnp.max(lg2, axis=-1, keepdims=True)
        i2 = jnp.min(jnp.where(lg2 == m2, lane_f, float(LANES)), axis=-1, keepdims=True)
        e = jnp.exp(m2 - m1)
        w1 = 1.0 / (1.0 + e)
        w2 = e / (1.0 + e)
        route_ref[...] = jnp.where(lane == 0, i1, jnp.where(lane == 1, i2, jnp.where(
            lane == 2, w1, jnp.where(lane == 3, w2, 0.0))))


def _merge(o_sb, o_mla, o_diff, proj, x2, mod, wb, wo, ng, wr, seq, moe):
    t, d = x2.shape
    tm = min(512, seq)
    per_b = seq // tm
    gcol = COL_GATE // d
    in_specs = [pl.BlockSpec((tm, BRANCH_WIDTH), lambda i: (i, 0)),
                pl.BlockSpec((tm, BRANCH_WIDTH), lambda i: (i, 0)),
                pl.BlockSpec((tm, BRANCH_WIDTH), lambda i: (i, 0)),
                pl.BlockSpec((tm, d), lambda i: (i, gcol)),
                pl.BlockSpec((tm, d), lambda i: (i, gcol + 1)),
                pl.BlockSpec((tm, d), lambda i: (i, gcol + 2)),
                pl.BlockSpec((tm, d), lambda i: (i, 0)),
                pl.BlockSpec((1, N_ADA, d), lambda i: (i // per_b, 0, 0)),
                pl.BlockSpec((N_BRANCH, BRANCH_WIDTH, d), lambda i: (0, 0, 0)),
                pl.BlockSpec((d, d), lambda i: (0, 0)),
                pl.BlockSpec((1, d), lambda i: (0, 0))]
    args = [o_sb, o_mla, o_diff, proj, proj, proj, x2, mod, wb, wo, ng]
    out_shape = [jax.ShapeDtypeStruct((t, d), F32), jax.ShapeDtypeStruct((t, d), F32 if moe else BF16)]
    out_specs = [pl.BlockSpec((tm, d), lambda i: (i, 0)), pl.BlockSpec((tm, d), lambda i: (i, 0))]
    if moe:
        in_specs.append(pl.BlockSpec((d, 2 * LANES), lambda i: (0, 0)))
        args.append(wr)
        out_shape.append(jax.ShapeDtypeStruct((t, LANES), F32))
        out_specs.append(pl.BlockSpec((tm, LANES), lambda i: (i, 0)))
    return pl.pallas_call(
        functools.partial(_merge_kernel, moe=moe),
        out_shape=tuple(out_shape),
        grid=(t // tm,),
        in_specs=in_specs,
        out_specs=tuple(out_specs),
        compiler_params=_cparams("arbitrary"),
        name="merge_moe" if moe else "merge_dense",
    )(*args)


def _swiglu_partial(h, wg, wu, wd):
    g = _dot(h, wg)
    u = _dot(h, wu)
    return _dot((g * jax.nn.sigmoid(g) * u).astype(BF16), wd)


def _ffn_kernel(h_ref, wg_ref, wu_ref, wd_ref, x_ref, mod_ref, o_ref):
    f = _swiglu_partial(h_ref[...], wg_ref[...], wu_ref[...], wd_ref[...])
    o_ref[...] = x_ref[...] + mod_ref[0, 5:6, :] * f


def _ffn(h, wg, wu, wd, x2, mod, seq):
    t, d = x2.shape
    d_ff = wg.shape[1]
    tm = min(512, seq)
    per_b = seq // tm
    resident = pl.Buffered(1)
    return pl.pallas_call(
        _ffn_kernel,
        out_shape=jax.ShapeDtypeStruct((t, d), F32),
        grid=(t // tm,),
        in_specs=[pl.BlockSpec((tm, d), lambda i: (i, 0)),
                  pl.BlockSpec((d, d_ff), lambda i: (0, 0), pipeline_mode=resident),
                  pl.BlockSpec((d, d_ff), lambda i: (0, 0), pipeline_mode=resident),
                  pl.BlockSpec((d_ff, d), lambda i: (0, 0), pipeline_mode=resident),
                  pl.BlockSpec((tm, d), lambda i: (i, 0)),
                  pl.BlockSpec((1, N_ADA, d), lambda i: (i // per_b, 0, 0))],
        out_specs=pl.BlockSpec((tm, d), lambda i: (i, 0)),
        compiler_params=_cparams("arbitrary"),
        name="ffn_dense",
    )(h, wg, wu, wd, x2, mod)


def _gather_start(src_hbm, idx_ref, base, dst_ref, sem):
    for r in range(dst_ref.shape[0]):
        row = idx_ref[base + r]
        pltpu.make_async_copy(src_hbm.at[pl.ds(row, 1), :], dst_ref.at[pl.ds(r, 1), :], sem).start()


def _gather_wait(src_hbm, dst_ref, sem):
    pltpu.make_async_copy(src_hbm.at[pl.ds(0, dst_ref.shape[0]), :], dst_ref, sem).wait()


def _moe_kernel(te_ref, nu_ref, rt_ref, h_hbm, wg_ref, wu_ref, wd_ref, y_ref, xf_ref, xb_ref, sem):
    i = pl.program_id(0)
    n_tiles = pl.num_programs(0)
    tm = xb_ref.shape[0]
    valid = i < nu_ref[0]
    slot = i % 2

    @pl.when(i == 0)
    def _():
        _gather_start(h_hbm, rt_ref, 0, xf_ref.at[0], sem.at[0])

    _gather_wait(h_hbm, xf_ref.at[slot], sem.at[slot])
    xb_ref[...] = xf_ref[slot].astype(BF16)

    @pl.when(i + 1 < n_tiles)
    def _():
        _gather_start(h_hbm, rt_ref, (i + 1) * tm, xf_ref.at[1 - slot], sem.at[1 - slot])

    @pl.when(valid)
    def _():
        y_ref[...] = _swiglu_partial(xb_ref[...], wg_ref[0], wu_ref[0], wd_ref[0])

    @pl.when(jnp.logical_not(valid))
    def _():
        y_ref[...] = jnp.zeros_like(y_ref)


def _moe_experts(h, wg, wu, wd, tile_expert, n_used, row_token, tm):
    t, d = h.shape
    d_ff = wg.shape[2]
    n_tiles = tile_expert.shape[0]
    resident = pl.Buffered(1)
    grid_spec = pltpu.PrefetchScalarGridSpec(
        num_scalar_prefetch=3,
        grid=(n_tiles,),
        in_specs=[pl.BlockSpec(memory_space=pl.ANY),
                  pl.BlockSpec((1, d, d_ff), lambda i, te, nu, rt: (te[i], 0, 0), pipeline_mode=resident),
                  pl.BlockSpec((1, d, d_ff), lambda i, te, nu, rt: (te[i], 0, 0), pipeline_mode=resident),
                  pl.BlockSpec((1, d_ff, d), lambda i, te, nu, rt: (te[i], 0, 0), pipeline_mode=resident)],
        out_specs=pl.BlockSpec((tm, d), lambda i, te, nu, rt: (i, 0)),
        scratch_shapes=[pltpu.VMEM((2, tm, d), F32), pltpu.VMEM((tm, d), BF16), pltpu.SemaphoreType.DMA((2,))],
    )
    return pl.pallas_call(
        _moe_kernel,
        out_shape=jax.ShapeDtypeStruct((n_tiles * tm, d), F32),
        grid_spec=grid_spec,
        compiler_params=_cparams("arbitrary"),
        name="moe_experts",
    )(tile_expert, n_used, row_token, h, wg, wu, wd)


def _combine_kernel(slot_ref, y_hbm, route_ref, x_ref, mod_ref, o_ref, yb_ref, sem):
    i = pl.program_id(0)
    n = pl.num_programs(0)
    tm = x_ref.shape[0]
    rows = tm * TOP_K
    buf = i % 2

    @pl.when(i == 0)
    def _():
        _gather_start(y_hbm, slot_ref, 0, yb_ref.at[0], sem.at[0])

    @pl.when(i + 1 < n)
    def _():
        _gather_start(y_hbm, slot_ref, (i + 1) * rows, yb_ref.at[1 - buf], sem.at[1 - buf])

    _gather_wait(y_hbm, yb_ref.at[buf], sem.at[buf])
    route = route_ref[...]
    f = route[:, 2:3] * yb_ref[buf, 0:tm, :] + route[:, 3:4] * yb_ref[buf, tm:rows, :]
    o_ref[...] = x_ref[...] + mod_ref[0, 5:6, :] * f


def _combine(slot, y, route, x2, mod, seq):
    t, d = x2.shape
    tm = min(256, seq)
    per_b = seq // tm
    grid_spec = pltpu.PrefetchScalarGridSpec(
        num_scalar_prefetch=1,
        grid=(t // tm,),
        in_specs=[pl.BlockSpec(memory_space=pl.ANY),
                  pl.BlockSpec((tm, LANES), lambda i, sl: (i, 0)),
                  pl.BlockSpec((tm, d), lambda i, sl: (i, 0)),
                  pl.BlockSpec((1, N_ADA, d), lambda i, sl: (i // per_b, 0, 0))],
        out_specs=pl.BlockSpec((tm, d), lambda i, sl: (i, 0)),
        scratch_shapes=[pltpu.VMEM((2, TOP_K * tm, d), F32), pltpu.SemaphoreType.DMA((2,))],
    )
    slot_tiles = slot.reshape(t // tm, tm, TOP_K).transpose(0, 2, 1).reshape(-1)
    return pl.pallas_call(
        _combine_kernel,
        out_shape=jax.ShapeDtypeStruct((t, d), F32),
        grid_spec=grid_spec,
        compiler_params=_cparams("arbitrary"),
        name="moe_combine",
    )(slot_tiles, y, route, x2, mod)


def _routing_tables(route, tm):
    t = route.shape[0]
    p = t * TOP_K
    n_tiles = p // tm + N_EXPERTS
    e = route[:, :TOP_K].astype(jnp.int32).reshape(p)
    onehot = (e[:, None] == jnp.arange(N_EXPERTS, dtype=jnp.int32)[None, :]).astype(jnp.int32)
    csum = jnp.cumsum(onehot, axis=0)
    count = csum[-1]
    rank = jnp.take_along_axis(csum, e[:, None], axis=1)[:, 0] - 1
    tiles_e = (count + tm - 1) // tm
    tile_end = jnp.cumsum(tiles_e)
    tile_start = tile_end - tiles_e
    slot = tile_start[e] * tm + rank
    n_used = tile_end[-1]
    tile_id = jnp.minimum(jnp.arange(n_tiles, dtype=jnp.int32), n_used - 1)
    tile_expert = jnp.sum((tile_id[:, None] >= tile_end[None, :]).astype(jnp.int32), axis=1)
    row_token = jnp.zeros((n_tiles * tm,), jnp.int32).at[slot].set(
        jnp.arange(p, dtype=jnp.int32) // TOP_K, unique_indices=True)
    return tile_expert.astype(jnp.int32), n_used.reshape(1).astype(jnp.int32), row_token, slot.astype(jnp.int32)


def _pack_w_in(w):
    d = w.shape[0]
    sb_cols = 3 * SB_HEADS * HALF
    mla0 = sb_cols
    dif0 = mla0 + MLA_Q_RANK + MLA_KV_RANK + MLA_ROPE
    gate0 = dif0 + 3 * BRANCH_WIDTH
    z = lambda n: jnp.zeros((d, n), w.dtype)
    packed = jnp.concatenate([
        w[:, :sb_cols],
        w[:, mla0:mla0 + MLA_Q_RANK + MLA_KV_RANK],
        w[:, dif0:gate0],
        w[:, gate0:],
        z(HALF), w[:, mla0 + MLA_Q_RANK + MLA_KV_RANK:dif0], z(LANES - HALF - MLA_ROPE)], axis=1)
    assert packed.shape[1] == IN_COLS_PACKED
    return packed.astype(BF16)


def _pack_w_uq(w):
    r = w.shape[0]
    half = MLA_ROPE // 2
    w3 = w.reshape(r, MLA_HEADS, MLA_QK)
    z = lambda n: jnp.zeros((r, MLA_HEADS, n), w.dtype)
    plain = jnp.concatenate([w3, z(LANES - MLA_QK)], axis=2)
    swapped = jnp.concatenate([z(MLA_NOPE), w3[:, :, MLA_NOPE + half:], w3[:, :, MLA_NOPE:MLA_NOPE + half],
                               z(LANES - MLA_QK)], axis=2)
    return jnp.concatenate([plain.reshape(r, MLA_HEADS * LANES), swapped.reshape(r, MLA_HEADS * LANES)],
                           axis=1).astype(BF16)


def _pack_w_ukv(w):
    r = w.shape[0]
    w3 = w.reshape(r, MLA_HEADS, MLA_NOPE + HALF)
    wk = jnp.concatenate([w3[:, :, :MLA_NOPE], jnp.zeros((r, MLA_HEADS, LANES - MLA_NOPE), w.dtype)], axis=2)
    wv = w3[:, :, MLA_NOPE:]
    return wk.reshape(r, MLA_HEADS * LANES).astype(BF16), wv.reshape(r, MLA_HEADS * HALF).astype(BF16)


def _head_lane_gain(g):
    return jnp.concatenate([g, jnp.zeros((LANES - MLA_QK,), g.dtype)])


def _prep_table(mla_q_gain, mla_k_gain, diff_q_gain, diff_k_gain):
    half = MLA_ROPE // 2
    inv_freq = ROPE_THETA ** (-jnp.arange(half, dtype=F32) / half)
    zeros = lambda n: jnp.zeros((n,), F32)
    freq = jnp.concatenate([zeros(MLA_NOPE), inv_freq, inv_freq, zeros(LANES - MLA_QK)])
    sgn_lo = jnp.concatenate([zeros(MLA_NOPE), -jnp.ones((half,), F32), zeros(LANES - MLA_NOPE - half)])
    sgn_hi = jnp.concatenate([zeros(MLA_NOPE + half), jnp.ones((half,), F32), zeros(LANES - MLA_QK)])
    log2e = math.log2(math.e)
    rows = [freq, sgn_lo, sgn_hi,
            _head_lane_gain(mla_q_gain) * (log2e / math.sqrt(MLA_QK)),
            _head_lane_gain(mla_k_gain),
            jnp.tile(diff_q_gain, 2) * (log2e / math.sqrt(HALF)),
            jnp.tile(diff_k_gain, 2),
            zeros(LANES)]
    return jnp.stack(rows).astype(F32)


def kernel(x, c, positions, norm1_g, norm2_g, w_ada, b_ada, w_in, mla_q_norm, w_mla_uq, mla_kv_norm,
           w_mla_ukv, mla_q_gain, mla_k_gain, diff_q_gain, diff_k_gain, diff_lambda, diff_subln,
           w_branch, w_out, w_ffn_gate, w_ffn_up, w_ffn_down, w_router, w_exp_gate, w_exp_up, w_exp_down):
    batch, seq, d = x.shape
    depth = w_in.shape[0]
    t = batch * seq
    blk = min(256, seq)
    x2 = x.reshape(t, d)
    pos_col = positions.reshape(t, 1).astype(jnp.int32)
    pos_blk = positions.reshape(batch, seq // blk, 1, blk).astype(jnp.int32)
    slopes = 2.0 ** (-8.0 * jnp.arange(1, DIFF_HEADS + 1, dtype=F32) / DIFF_HEADS)
    mods = _ada(c, w_ada, b_ada).reshape(depth, batch, N_ADA, d)
    moe_tm = min(512, seq)

    for layer in range(depth):
        mod = mods[layer]
        proj = _in_proj(x2, norm1_g[layer].reshape(1, d), mod, _pack_w_in(w_in[layer]), seq)
        wuk, wuv = _pack_w_ukv(w_mla_ukv[layer])
        tab = _prep_table(mla_q_gain[layer], mla_k_gain[layer], diff_q_gain[layer], diff_k_gain[layer])
        qm, km, vm, dqn, dkn = _prep(proj, pos_col, tab, mla_q_norm[layer].reshape(1, -1),
                                     mla_kv_norm[layer].reshape(1, -1), _pack_w_uq(w_mla_uq[layer]),
                                     wuk, wuv, batch, seq)
        o_sb = _sb_attention(proj, batch, seq)
        o_mla = _mla_attention(qm, km, vm, batch, seq)
        lam_init = 0.8 - 0.6 * math.exp(-0.3 * layer)
        o_diff = _diff_attention(dqn, dkn, proj, pos_col, pos_blk, diff_lambda[layer],
                                 diff_subln[layer].reshape(1, LANES), slopes, lam_init, batch, seq)
        j = layer // 2
        moe = layer % 2 == 1
        wr = None
        if moe:
            wr = jnp.concatenate([w_router[j], jnp.zeros((d, LANES - N_EXPERTS), F32)], axis=1)
            wr_hi = wr.astype(BF16)
            wr = jnp.concatenate([wr_hi, (wr - wr_hi.astype(F32)).astype(BF16)], axis=1)
        outs = _merge(o_sb, o_mla, o_diff, proj, x2, mod, w_branch[layer].astype(BF16),
                      w_out[layer].astype(BF16), norm2_g[layer].reshape(1, d), wr, seq, moe)
        if moe:
            x_mid, h2, route = outs
            tile_expert, n_used, row_token, slot = _routing_tables(route, moe_tm)
            y = _moe_experts(h2, w_exp_gate[j].astype(BF16), w_exp_up[j].astype(BF16),
                             w_exp_down[j].astype(BF16), tile_expert, n_used, row_token, moe_tm)
            x2 = _combine(slot, y, route, x_mid, mod, seq)
        else:
            x_mid, h2 = outs
            x2 = _ffn(h2, w_ffn_gate[j].astype(BF16), w_ffn_up[j].astype(BF16),
                      w_ffn_down[j].astype(BF16), x_mid, mod, seq)
    return x2.reshape(batch, seq, d)
```
